```python
import jax, jax.numpy as jnp
from jax import lax
import numpy as np

D_MODEL = 2048
BATCH = 1
SEQ = 8192
DEPTH = 1
DEC_BATCH = 2
DEC_SEQ = 16384
PAST_LEN = 128

GRID_W = 64
MIX_WIDTH = D_MODEL
ATTN_WIDTH = MIX_WIDTH // 2
POOL_WIDTH = MIX_WIDTH - ATTN_WIDTH
HEAD_DIM = 32
N_HEADS = ATTN_WIDTH // HEAD_DIM
WIN_ROWS = 8
WIN_COLS = 16
POOL_WINDOWS = (2, 4, 8, 16)
N_POOL_GROUPS = len(POOL_WINDOWS)
POOL_GROUP = POOL_WIDTH // N_POOL_GROUPS
IN_WIDTH = 3 * ATTN_WIDTH + POOL_WIDTH
N_EXPERTS = 32
TOP_K = 4
D_FF = D_MODEL
SWIGLU_LIMIT = 7.0
SWIGLU_ALPHA = 1.702
MOE_BLOCK = 256
LN_EPS = 1e-5
DEEPNORM_ALPHA = float((2 * DEPTH) ** 0.25)
DEEPNORM_BETA = float((8 * DEPTH) ** -0.25)

kernel_name = "hybrid_natten_pool_moe_encoder"


def layer_norm(x, g, b):
    xf = x.astype(jnp.float32)
    mu = jnp.mean(xf, axis=-1, keepdims=True)
    var = jnp.mean(jnp.square(xf - mu), axis=-1, keepdims=True)
    return ((xf - mu) * lax.rsqrt(var + LN_EPS) * g.astype(jnp.float32) + b.astype(jnp.float32)).astype(x.dtype)


def neighbourhood_attention(q, k, v, rpb):
    bsz, seq_len, n_heads, head_dim = q.shape
    rows = seq_len // GRID_W
    kh = min(WIN_ROWS, rows)
    kw = WIN_COLS
    cols = jnp.arange(GRID_W)
    col_start = jnp.clip(cols - kw // 2, 0, GRID_W - kw)
    col_idx = col_start[:, None] + jnp.arange(kw)[None, :]
    dc = col_idx - cols[:, None] + (WIN_COLS - 1)
    qg = q.reshape(bsz, rows, GRID_W, n_heads, head_dim) * (head_dim ** -0.5)
    kg = k.reshape(bsz, rows, GRID_W, n_heads, head_dim)
    vg = v.reshape(bsz, rows, GRID_W, n_heads, head_dim)

    def one_row(r):
        rs = jnp.clip(r - kh // 2, 0, rows - kh)
        qr = lax.dynamic_index_in_dim(qg, r, axis=1, keepdims=False)
        kb = lax.dynamic_slice_in_dim(kg, rs, kh, axis=1)
        vb = lax.dynamic_slice_in_dim(vg, rs, kh, axis=1)
        kn = kb[:, :, col_idx]
        vn = vb[:, :, col_idx]
        dr = rs + jnp.arange(kh) - r + (WIN_ROWS - 1)
        bias = rpb[:, dr[:, None, None], dc[None, :, :]]
        bias = jnp.transpose(bias, (0, 2, 1, 3)).astype(jnp.float32)
        s = jnp.einsum('bqhd,brqwhd->bhqrw', qr, kn).astype(jnp.float32) + bias[None]
        p = jax.nn.softmax(s, axis=(-2, -1)).astype(vn.dtype)
        return jnp.einsum('bhqrw,brqwhd->bqhd', p, vn)

    out = lax.map(one_row, jnp.arange(rows))
    out = jnp.transpose(out, (1, 0, 2, 3, 4))
    return out.reshape(bsz, seq_len, n_heads * head_dim)


def multiscale_pool(u, w_pool, pool_scale):
    bsz, seq_len, _ = u.shape
    uf = u.astype(jnp.float32).reshape(bsz, seq_len, N_POOL_GROUPS, POOL_GROUP)
    cs = jnp.concatenate([jnp.zeros((bsz, 1, N_POOL_GROUPS, POOL_GROUP), jnp.float32),
                          lax.cumsum(uf, axis=1)], axis=1)
    t = jnp.arange(seq_len)[:, None]
    half = jnp.array([w // 2 for w in POOL_WINDOWS], dtype=jnp.int32)[None, :]
    lo = jnp.clip(t - half, 0, seq_len)
    hi = jnp.clip(t + half, 0, seq_len)
    gidx = jnp.arange(N_POOL_GROUPS)[None, :]
    window_sum = cs[:, hi, gidx] - cs[:, lo, gidx]
    cnt = (hi - lo).astype(jnp.float32)[None, :, :, None]
    pooled = window_sum / cnt - uf
    mixed = jnp.einsum('blgc,gcd->blgd', pooled, w_pool.astype(jnp.float32))
    mixed = mixed * pool_scale.astype(jnp.float32).reshape(N_POOL_GROUPS, POOL_GROUP)
    return mixed.reshape(bsz, seq_len, POOL_WIDTH).astype(u.dtype)


def moe(h, w_router, b_router, w_gate_up, b_gate_up, w_down, b_down):
    bsz, seq_len, d = h.shape
    x = h.reshape(-1, d)
    n = x.shape[0]
    logits = (x @ w_router + b_router).astype(jnp.float32)
    top_val, top_idx = lax.top_k(logits, TOP_K)
    gates = jax.nn.softmax(top_val, axis=-1)
    n_assign = n * TOP_K
    e_flat = top_idx.reshape(-1).astype(jnp.int32)
    g_flat = gates.reshape(-1)
    tok_flat = jnp.arange(n_assign, dtype=jnp.int32) // TOP_K
    order = jnp.argsort(e_flat)
    e_sorted = e_flat[order]
    counts = jnp.zeros((N_EXPERTS,), jnp.int32).at[e_flat].add(1)
    padded = (counts + MOE_BLOCK - 1) // MOE_BLOCK * MOE_BLOCK
    off = jnp.cumsum(counts) - counts
    pend = jnp.cumsum(padded)
    poff = pend - padded
    dest = poff[e_sorted] + (jnp.arange(n_assign, dtype=jnp.int32) - off[e_sorted])
    n_slots = n_assign + N_EXPERTS * MOE_BLOCK
    n_blocks = n_slots // MOE_BLOCK
    slot_tok = jnp.full((n_slots,), n, jnp.int32).at[dest].set(tok_flat[order])
    slot_gate = jnp.zeros((n_slots,), jnp.float32).at[dest].set(g_flat[order])
    block_start = jnp.arange(n_blocks, dtype=jnp.int32) * MOE_BLOCK
    block_e = jnp.minimum(jnp.sum(block_start[:, None] >= pend[None, :], axis=1), N_EXPERTS - 1).astype(jnp.int32)
    x_pad = jnp.concatenate([x, jnp.zeros((1, d), x.dtype)], axis=0)

    def step(acc, blk):
        tok, gate, e = blk
        xb = x_pad[tok]
        gu = xb @ w_gate_up[e] + b_gate_up[e]
        g_lin = jnp.minimum(gu[:, :D_FF], SWIGLU_LIMIT)
        up = jnp.clip(gu[:, D_FF:], -SWIGLU_LIMIT, SWIGLU_LIMIT)
        act = (up + 1.0) * (g_lin * jax.nn.sigmoid(SWIGLU_ALPHA * g_lin))
        yb = act @ w_down[e] + b_down[e]
        return acc.at[tok].add(yb.astype(jnp.float32) * gate[:, None]), None

    acc, _ = lax.scan(step, jnp.zeros((n + 1, d), jnp.float32),
                      (slot_tok.reshape(n_blocks, MOE_BLOCK), slot_gate.reshape(n_blocks, MOE_BLOCK), block_e))
    return acc[:n].reshape(bsz, seq_len, d).astype(h.dtype)


def encoder_layer(x, w_in, rpb, w_pool, pool_scale, w_out, ln1_g, ln1_b,
                  w_router, b_router, w_gate_up, b_gate_up, w_down, b_down, ln2_g, ln2_b):
    bsz, seq_len, _ = x.shape
    proj = x @ w_in
    q = proj[..., :ATTN_WIDTH].reshape(bsz, seq_len, N_HEADS, HEAD_DIM)
    k = proj[..., ATTN_WIDTH:2 * ATTN_WIDTH].reshape(bsz, seq_len, N_HEADS, HEAD_DIM)
    v = proj[..., 2 * ATTN_WIDTH:3 * ATTN_WIDTH].reshape(bsz, seq_len, N_HEADS, HEAD_DIM)
    u = proj[..., 3 * ATTN_WIDTH:]
    attn = neighbourhood_attention(q, k, v, rpb)
    pool = multiscale_pool(u, w_pool, pool_scale)
    mix = jnp.concatenate([attn, pool], axis=-1) @ w_out
    h = layer_norm(DEEPNORM_ALPHA * x + mix, ln1_g, ln1_b)
    f = moe(h, w_router, b_router, w_gate_up, b_gate_up, w_down, b_down)
    return layer_norm(DEEPNORM_ALPHA * h + f, ln2_g, ln2_b)


def trunk(x, ln_in_g, ln_in_b, w_in, rpb, w_pool, pool_scale, w_out, ln1_g, ln1_b,
          w_router, b_router, w_gate_up, b_gate_up, w_down, b_down, ln2_g, ln2_b):
    h = layer_norm(x, ln_in_g, ln_in_b)
    for l in range(DEPTH):
        h = encoder_layer(h, w_in[l], rpb[l], w_pool[l], pool_scale[l], w_out[l], ln1_g[l], ln1_b[l],
                          w_router[l], b_router[l], w_gate_up[l], b_gate_up[l], w_down[l], b_down[l],
                          ln2_g[l], ln2_b[l])
    return h


def setup_inputs(seed: int = 0) -> dict:
    key = jax.random.key(seed)
    ks = jax.random.split(key, 20)
    f32 = jnp.float32
    nrm = lambda k, shape: jax.random.normal(k, shape, f32)
    x_prompt = nrm(ks[0], (BATCH, SEQ, D_MODEL))
    x_sample = nrm(ks[1], (DEC_BATCH, DEC_SEQ, D_MODEL))
    ln_in_g = 1.0 + 0.1 * nrm(ks[2], (D_MODEL,))
    ln_in_b = 0.01 * nrm(ks[3], (D_MODEL,))
    col_scale = jnp.concatenate([jnp.ones((2 * ATTN_WIDTH,), f32),
                                 jnp.full((ATTN_WIDTH,), DEEPNORM_BETA, f32),
                                 jnp.ones((POOL_WIDTH,), f32)])
    w_in = nrm(ks[4], (DEPTH, D_MODEL, IN_WIDTH)) * (D_MODEL ** -0.5) * col_scale
    rpb = 0.02 * nrm(ks[5], (DEPTH, N_HEADS, 2 * WIN_ROWS - 1, 2 * WIN_COLS - 1))
    w_pool = nrm(ks[6], (DEPTH, N_POOL_GROUPS, POOL_GROUP, POOL_GROUP)) * (POOL_GROUP ** -0.5)
    pool_scale = 1.0 + 0.1 * nrm(ks[7], (DEPTH, POOL_WIDTH))
    w_out = nrm(ks[8], (DEPTH, MIX_WIDTH, D_MODEL)) * (MIX_WIDTH ** -0.5) * DEEPNORM_BETA
    ln1_g = 1.0 + 0.1 * nrm(ks[9], (DEPTH, D_MODEL))
    ln1_b = 0.01 * nrm(ks[10], (DEPTH, D_MODEL))
    w_router = nrm(ks[11], (DEPTH, D_MODEL, N_EXPERTS)) * (D_MODEL ** -0.5)
    b_router = 0.01 * nrm(ks[12], (DEPTH, N_EXPERTS))
    w_gate_up = nrm(ks[13], (DEPTH, N_EXPERTS, D_MODEL, 2 * D_FF)) * (D_MODEL ** -0.5)
    b_gate_up = 0.01 * nrm(ks[14], (DEPTH, N_EXPERTS, 2 * D_FF))
    w_down = nrm(ks[15], (DEPTH, N_EXPERTS, D_FF, D_MODEL)) * (D_FF ** -0.5) * DEEPNORM_BETA
    b_down = 0.01 * nrm(ks[16], (DEPTH, N_EXPERTS, D_MODEL))
    ln2_g = 1.0 + 0.1 * nrm(ks[17], (DEPTH, D_MODEL))
    ln2_b = 0.01 * nrm(ks[18], (DEPTH, D_MODEL))
    return {"x_prompt": x_prompt, "x_sample": x_sample, "ln_in_g": ln_in_g, "ln_in_b": ln_in_b,
            "w_in": w_in, "rpb": rpb, "w_pool": w_pool, "pool_scale": pool_scale, "w_out": w_out,
            "ln1_g": ln1_g, "ln1_b": ln1_b, "w_router": w_router, "b_router": b_router,
            "w_gate_up": w_gate_up, "b_gate_up": b_gate_up, "w_down": w_down, "b_down": b_down,
            "ln2_g": ln2_g, "ln2_b": ln2_b}


def reference(x_prompt, x_sample, ln_in_g, ln_in_b, w_in, rpb, w_pool, pool_scale, w_out, ln1_g, ln1_b,
              w_router, b_router, w_gate_up, b_gate_up, w_down, b_down, ln2_g, ln2_b):
    y_prompt = trunk(x_prompt, ln_in_g, ln_in_b, w_in, rpb, w_pool, pool_scale, w_out, ln1_g, ln1_b,
                     w_router, b_router, w_gate_up, b_gate_up, w_down, b_down, ln2_g, ln2_b)
    y_sample = trunk(x_sample, ln_in_g, ln_in_b, w_in, rpb, w_pool, pool_scale, w_out, ln1_g, ln1_b,
                     w_router, b_router, w_gate_up, b_gate_up, w_down, b_down, ln2_g, ln2_b)
    return (y_prompt, y_sample)
```

```python
import functools

import jax
import jax.numpy as jnp
from jax import lax
from jax.experimental import pallas as pl
from jax.experimental.pallas import tpu as pltpu

F32 = jnp.float32
BF16 = jnp.bfloat16

LANES = 128
VMEM_LIMIT_BYTES = 58 * 1024 * 1024

GRID_W = 64
HEAD_DIM = 32
WIN_ROWS = 8
WIN_COLS = 16
POOL_WINDOWS = (2, 4, 8, 16)
TOP_K = 4
SWIGLU_LIMIT = 7.0
SWIGLU_ALPHA = 1.702
LN_EPS = 1e-5
NEG_BIG = -1e30

HEADS_PER_GROUP = LANES // HEAD_DIM
POOL_HALO = 16


def _layer_norm(x, g, b):
    mu = jnp.mean(x, axis=-1, keepdims=True)
    xc = x - mu
    var = jnp.mean(xc * xc, axis=-1, keepdims=True)
    return xc * lax.rsqrt(var + LN_EPS) * g + b


def _params(semantics):
    return pltpu.CompilerParams(dimension_semantics=semantics, vmem_limit_bytes=VMEM_LIMIT_BYTES)


def _in_proj_kernel(xs_ref, xp_ref, g_ref, b_ref, w_ref, q_ref, k_ref, v_ref, u_ref, *, n_sample_blocks,
                    attn_width, q_scale):
    i = pl.program_id(0)
    x = jnp.where(i < n_sample_blocks, xs_ref[...], xp_ref[...])
    h = _layer_norm(x, g_ref[...], b_ref[...])
    proj = jnp.dot(h.astype(BF16), w_ref[...], preferred_element_type=F32)
    a = attn_width
    q_ref[...] = (proj[:, :a] * q_scale).astype(BF16)
    k_ref[...] = proj[:, a:2 * a].astype(BF16)
    v_ref[...] = proj[:, 2 * a:3 * a].astype(BF16)
    u_ref[...] = proj[:, 3 * a:].astype(BF16)


def _in_proj(xs, xp, ln_g, ln_b, w_in_bf16, *, attn_width, block):
    ns, d = xs.shape
    np_ = xp.shape[0]
    n = ns + np_
    in_width = w_in_bf16.shape[1]
    pool_width = in_width - 3 * attn_width
    nsb = ns // block
    kern = functools.partial(_in_proj_kernel, n_sample_blocks=nsb, attn_width=attn_width,
                             q_scale=HEAD_DIM ** -0.5)
    row = lambda i: (i, 0)
    const = lambda i: (0, 0)
    return pl.pallas_call(
        kern,
        grid=(n // block,),
        in_specs=[
            pl.BlockSpec((block, d), lambda i: (jnp.minimum(i, nsb - 1), 0)),
            pl.BlockSpec((block, d), lambda i: (jnp.maximum(i - nsb, 0), 0)),
            pl.BlockSpec((1, d), const),
            pl.BlockSpec((1, d), const),
            pl.BlockSpec((d, in_width), const),
        ],
        out_specs=[
            pl.BlockSpec((block, attn_width), row),
            pl.BlockSpec((block, attn_width), row),
            pl.BlockSpec((block, attn_width), row),
            pl.BlockSpec((block, pool_width), row),
        ],
        out_shape=[
            jax.ShapeDtypeStruct((n, attn_width), BF16),
            jax.ShapeDtypeStruct((n, attn_width), BF16),
            jax.ShapeDtypeStruct((n, attn_width), BF16),
            jax.ShapeDtypeStruct((n, pool_width), BF16),
        ],
        compiler_params=_params(("arbitrary",)),
        name="in_proj",
    )(xs, xp, ln_g, ln_b, w_in_bf16)


def _attn_bias_table(rpb):
    n_heads = rpb.shape[0]
    cols = jnp.arange(GRID_W)
    col_start = jnp.clip(cols - WIN_COLS // 2, 0, GRID_W - WIN_COLS)
    ck = jnp.arange(GRID_W)
    in_win = (ck[None, :] >= col_start[:, None]) & (ck[None, :] < col_start[:, None] + WIN_COLS)
    dc = jnp.clip(ck[None, :] - cols[:, None] + (WIN_COLS - 1), 0, 2 * WIN_COLS - 2)
    case = jnp.arange(WIN_ROWS)
    dr = jnp.arange(WIN_ROWS)[None, :] + (WIN_ROWS - 1) - case[:, None]
    t = rpb.astype(F32)[:, dr[:, :, None, None], dc[None, None, :, :]]
    t = jnp.where(in_win[None, None, None], t, NEG_BIG)
    t = jnp.transpose(t, (0, 1, 3, 2, 4))
    g = n_heads // HEADS_PER_GROUP
    t = t.reshape(g, HEADS_PER_GROUP, WIN_ROWS, GRID_W, WIN_ROWS * GRID_W)
    t = jnp.transpose(t, (0, 2, 1, 3, 4))
    return t.reshape(g, WIN_ROWS, HEADS_PER_GROUP * GRID_W, WIN_ROWS * GRID_W)


def _attention_kernel(q_ref, k_ref, v_ref, bias_ref, o_ref, *, rows):
    lane_head = lax.broadcasted_iota(jnp.int32, (1, LANES), 1) // HEAD_DIM
    kv_rows = WIN_ROWS * GRID_W

    def body(r, carry):
        rs = jnp.clip(r - WIN_ROWS // 2, 0, rows - WIN_ROWS)
        case = jnp.minimum(r, WIN_ROWS // 2) + jnp.maximum(r - (rows - WIN_ROWS // 2), 0)
        q = q_ref[pl.ds(pl.multiple_of(r * GRID_W, GRID_W), GRID_W), :]
        kb = k_ref[pl.ds(pl.multiple_of(rs * GRID_W, GRID_W), kv_rows), :]
        vb = v_ref[pl.ds(pl.multiple_of(rs * GRID_W, GRID_W), kv_rows), :]
        zero = jnp.zeros_like(q)
        qs = jnp.concatenate([jnp.where(lane_head == h, q, zero) for h in range(HEADS_PER_GROUP)], axis=0)
        s = lax.dot_general(qs, kb, (((1,), (1,)), ((), ())), preferred_element_type=F32)
        s = s + bias_ref[0, case]
        m = jnp.max(s, axis=-1, keepdims=True)
        p = jnp.exp(s - m)
        l = jnp.sum(p, axis=-1, keepdims=True)
        o = jnp.dot(p.astype(BF16), vb, preferred_element_type=F32) / l
        out = jnp.zeros((GRID_W, LANES), F32)
        for h in range(HEADS_PER_GROUP):
            out = jnp.where(lane_head == h, o[h * GRID_W:(h + 1) * GRID_W], out)
        o_ref[pl.ds(pl.multiple_of(r * GRID_W, GRID_W), GRID_W), :] = out.astype(o_ref.dtype)
        return carry

    lax.fori_loop(0, rows, body, 0)


def _attention(q, k, v, bias, *, seq_len, n_seq, first_block):
    a = q.shape[1]
    groups = a // LANES
    rows = seq_len // GRID_W
    assert rows >= WIN_ROWS and seq_len % GRID_W == 0
    kern = functools.partial(_attention_kernel, rows=rows)
    tok = lambda b, g: (first_block + b, g)
    return pl.pallas_call(
        kern,
        grid=(n_seq, groups),
        in_specs=[
            pl.BlockSpec((seq_len, LANES), tok),
            pl.BlockSpec((seq_len, LANES), tok),
            pl.BlockSpec((seq_len, LANES), tok),
            pl.BlockSpec((1,) + bias.shape[1:], lambda b, g: (g, 0, 0, 0)),
        ],
        out_specs=pl.BlockSpec((seq_len, LANES), lambda b, g: (b, g)),
        out_shape=jax.ShapeDtypeStruct((n_seq * seq_len, a), BF16),
        compiler_params=_params(("arbitrary", "arbitrary")),
        name="nbr_attention",
    )(q, k, v, bias)


def _mix_kernel(attn_s_ref, attn_p_ref, u_ref, utop_ref, ubot_ref, xs_ref, xp_ref, lng_ref, lnb_ref, wpool_ref,
                pscale_ref, wout_ref, g1_ref, b1_ref, wrh_ref, wrl_ref, br_ref,
                h_ref, hs_ref, logit_ref, ubuf, *, n_sample_blocks, sample_len, n_sample_tokens, prompt_len,
                alpha):
    i = pl.program_id(0)
    tb = u_ref.shape[0]
    d = h_ref.shape[1]
    a = attn_s_ref.shape[1]
    pg = wpool_ref.shape[1]
    is_sample = i < n_sample_blocks
    tok0 = i * tb
    seq_len = jnp.where(is_sample, sample_len, prompt_len)
    local0 = jnp.where(is_sample, lax.rem(tok0, sample_len), tok0 - n_sample_tokens)
    top_ok = local0 > 0
    bot_ok = local0 + tb < seq_len

    ubuf[0:POOL_HALO, :] = jnp.where(top_ok, utop_ref[...].astype(F32), 0.0)
    ubuf[POOL_HALO:POOL_HALO + tb, :] = u_ref[...].astype(F32)
    ubuf[POOL_HALO + tb:, :] = jnp.where(bot_ok, ubot_ref[...].astype(F32), 0.0)
    pos = local0 + lax.broadcasted_iota(jnp.int32, (tb, 1), 0)

    attn = jnp.where(is_sample, attn_s_ref[...], attn_p_ref[...])
    mix = jnp.dot(attn, wout_ref[0:a, :], preferred_element_type=F32)
    for g, w in enumerate(POOL_WINDOWS):
        half = w // 2
        c0 = g * pg
        win = ubuf[POOL_HALO - half:POOL_HALO - half + tb, c0:c0 + pg]
        for dlt in range(-half + 1, half):
            win = win + ubuf[POOL_HALO + dlt:POOL_HALO + dlt + tb, c0:c0 + pg]
        cnt = (jnp.minimum(pos + half, seq_len) - jnp.maximum(pos - half, 0)).astype(F32)
        pooled = win / cnt - ubuf[POOL_HALO:POOL_HALO + tb, c0:c0 + pg]
        mixed = jnp.dot(pooled.astype(BF16), wpool_ref[g], preferred_element_type=F32)
        mixed = mixed * pscale_ref[:, c0:c0 + pg]
        mix = mix + jnp.dot(mixed.astype(BF16), wout_ref[a + c0:a + c0 + pg, :], preferred_element_type=F32)

    x = jnp.where(is_sample, xs_ref[...], xp_ref[...])
    h0 = _layer_norm(x, lng_ref[...], lnb_ref[...])
    h = _layer_norm(alpha * h0 + mix, g1_ref[...], b1_ref[...])
    h_ref[...] = h
    slabs = d // LANES
    for kk in range(slabs):
        hs_ref[pl.ds(kk, tb, stride=slabs), :] = h[:, kk * LANES:(kk + 1) * LANES]
    hi = h.astype(BF16)
    lo = (h - hi.astype(F32)).astype(BF16)
    logits = jnp.dot(hi, wrh_ref[...], preferred_element_type=F32)
    logits = logits + (jnp.dot(lo, wrh_ref[...], preferred_element_type=F32)
                       + jnp.dot(hi, wrl_ref[...], preferred_element_type=F32))
    logit_ref[...] = logits + br_ref[...]


def _mix(attn_s, attn_p, u, xs, xp, ln_g, ln_b, w_pool_bf16, pool_scale, w_out_bf16, g1, b1, wr_hi, wr_lo, br,
         *, sample_len, prompt_len, alpha, block):
    ns, d = xs.shape
    np_ = xp.shape[0]
    n = ns + np_
    a = attn_s.shape[1]
    p = u.shape[1]
    nsb = ns // block
    slabs = d // LANES
    halo_blocks = block // POOL_HALO
    n_halo = n // POOL_HALO
    kern = functools.partial(_mix_kernel, n_sample_blocks=nsb, sample_len=sample_len, n_sample_tokens=ns,
                             prompt_len=prompt_len, alpha=alpha)
    row = lambda i: (i, 0)
    const = lambda i: (0, 0)
    srow = lambda i: (jnp.minimum(i, nsb - 1), 0)
    prow = lambda i: (jnp.maximum(i - nsb, 0), 0)
    return pl.pallas_call(
        kern,
        grid=(n // block,),
        in_specs=[
            pl.BlockSpec((block, a), srow),
            pl.BlockSpec((block, a), prow),
            pl.BlockSpec((block, p), row),
            pl.BlockSpec((POOL_HALO, p), lambda i: (jnp.maximum(i * halo_blocks - 1, 0), 0)),
            pl.BlockSpec((POOL_HALO, p), lambda i: (jnp.minimum((i + 1) * halo_blocks, n_halo - 1), 0)),
            pl.BlockSpec((block, d), srow),
            pl.BlockSpec((block, d), prow),
            pl.BlockSpec((1, d), const),
            pl.BlockSpec((1, d), const),
            pl.BlockSpec(w_pool_bf16.shape, lambda i: (0, 0, 0)),
            pl.BlockSpec((1, p), const),
            pl.BlockSpec(w_out_bf16.shape, const),
            pl.BlockSpec((1, d), const),
            pl.BlockSpec((1, d), const),
            pl.BlockSpec(wr_hi.shape, const),
            pl.BlockSpec(wr_lo.shape, const),
            pl.BlockSpec((1, LANES), const),
        ],
        out_specs=[
            pl.BlockSpec((block, d), row),
            pl.BlockSpec((block * slabs, LANES), row),
            pl.BlockSpec((block, LANES), row),
        ],
        out_shape=[
            jax.ShapeDtypeStruct((n, d), F32),
            jax.ShapeDtypeStruct((n * slabs, LANES), F32),
            jax.ShapeDtypeStruct((n, LANES), F32),
        ],
        scratch_shapes=[pltpu.VMEM((block + 2 * POOL_HALO, p), F32)],
        compiler_params=_params(("arbitrary",)),
        name="mix_norm_logits",
    )(attn_s, attn_p, u, u, u, xs, xp, ln_g, ln_b, w_pool_bf16, pool_scale, w_out_bf16, g1, b1, wr_hi, wr_lo, br)


def _route_kernel(logit_ref, meta_ref, gate_ref, count_ref, carry, *, n_experts):
    i = pl.program_id(0)
    tb = logit_ref.shape[0]

    @pl.when(i == 0)
    def _():
        carry[...] = jnp.zeros_like(carry)

    lane = lax.broadcasted_iota(jnp.int32, (tb, LANES), 1)
    lane_f = lane.astype(F32)
    cur = jnp.where(lane < n_experts, logit_ref[...], -jnp.inf)
    vals, idxs, hots = [], [], []
    for _ in range(TOP_K):
        m = jnp.max(cur, axis=-1, keepdims=True)
        idx = jnp.min(jnp.where(cur == m, lane_f, float(LANES)), axis=-1, keepdims=True)
        hot = lane_f == idx
        cur = jnp.where(hot, -jnp.inf, cur)
        vals.append(m)
        idxs.append(idx)
        hots.append(hot)
    exps = [jnp.exp(v - vals[0]) for v in vals]
    den = exps[0]
    for e in exps[1:]:
        den = den + e
    chosen = jnp.zeros((tb, LANES), F32)
    for hot in hots:
        chosen = jnp.where(hot, 1.0, chosen)
    r_io = lax.broadcasted_iota(jnp.int32, (tb, tb), 0)
    c_io = lax.broadcasted_iota(jnp.int32, (tb, tb), 1)
    tri = jnp.where(r_io > c_io, 1.0, 0.0).astype(BF16)
    before = jnp.dot(tri, chosen.astype(BF16), preferred_element_type=F32) + carry[...]
    meta = jnp.zeros((tb, LANES), F32)
    gates = jnp.zeros((tb, LANES), F32)
    for k in range(TOP_K):
        rank = jnp.sum(jnp.where(hots[k], before, 0.0), axis=-1, keepdims=True)
        meta = jnp.where(lane == k, idxs[k], meta)
        meta = jnp.where(lane == TOP_K + k, rank, meta)
        gates = jnp.where(lane == k, exps[k] / den, gates)
    meta_ref[...] = meta.astype(jnp.int32)
    gate_ref[...] = gates
    total = carry[...] + jnp.sum(chosen, axis=0, keepdims=True)
    carry[...] = total
    count_ref[...] = total.astype(jnp.int32)


def _route(logits, *, n_experts, block):
    n = logits.shape[0]
    assert n * TOP_K < 2 ** 24
    row = lambda i: (i, 0)
    return pl.pallas_call(
        functools.partial(_route_kernel, n_experts=n_experts),
        grid=(n // block,),
        in_specs=[pl.BlockSpec((block, LANES), row)],
        out_specs=[
            pl.BlockSpec((block, LANES), row),
            pl.BlockSpec((block, LANES), row),
            pl.BlockSpec((1, LANES), lambda i: (0, 0)),
        ],
        out_shape=[
            jax.ShapeDtypeStruct((n, LANES), jnp.int32),
            jax.ShapeDtypeStruct((n, LANES), F32),
            jax.ShapeDtypeStruct((1, LANES), jnp.int32),
        ],
        scratch_shapes=[pltpu.VMEM((1, LANES), F32)],
        compiler_params=_params(("arbitrary",)),
        name="route_topk",
    )(logits)


def _row_copy(src_hbm, dst_vmem, sem, src_row, dst_row, slabs):
    return pltpu.make_async_copy(
        src_hbm.at[pl.ds(pl.multiple_of(src_row * slabs, slabs), slabs), :],
        dst_vmem.at[pl.ds(pl.multiple_of(dst_row * slabs, slabs), slabs), :],
        sem)


def _experts_kernel(te_ref, nv_ref, tok_ref, hs_hbm, wg_ref, wu_ref, bg_ref, bu_ref, wd_ref, bd_ref, ys_ref,
                    idx_smem, xbuf, xb16, acc, sem_idx, sem_rows):
    i = pl.program_id(0)
    j = pl.program_id(1)
    nj = pl.num_programs(1)
    nv = nv_ref[i]
    tm, d = xb16.shape
    slabs = d // LANES

    @pl.when(nv > 0)
    def _():
        @pl.when(j == 0)
        def _():
            cp = pltpu.make_async_copy(tok_ref.at[0, 0], idx_smem, sem_idx)
            cp.start()
            cp.wait()

            def issue(r, c):
                _row_copy(hs_hbm, xbuf, sem_rows, idx_smem[r], r, slabs).start()
                return c

            lax.fori_loop(0, tm, issue, 0)

            def drain(r, c):
                _row_copy(hs_hbm, xbuf, sem_rows, 0, r, slabs).wait()
                return c

            lax.fori_loop(0, tm, drain, 0)
            for kk in range(slabs):
                xb16[:, kk * LANES:(kk + 1) * LANES] = xbuf[pl.ds(kk, tm, stride=slabs), :].astype(BF16)

        x = xb16[...]
        gate = jnp.dot(x, wg_ref[0].astype(BF16), preferred_element_type=F32) + bg_ref[0]
        up = jnp.dot(x, wu_ref[0].astype(BF16), preferred_element_type=F32) + bu_ref[0]
        gate = jnp.minimum(gate, SWIGLU_LIMIT)
        up = jnp.clip(up, -SWIGLU_LIMIT, SWIGLU_LIMIT)
        act = (up + 1.0) * (gate * jax.nn.sigmoid(SWIGLU_ALPHA * gate))
        y = jnp.dot(act.astype(BF16), wd_ref[0].astype(BF16), preferred_element_type=F32)

        @pl.when(j == 0)
        def _():
            acc[...] = y + bd_ref[0]

        @pl.when(j > 0)
        def _():
            acc[...] += y

        @pl.when(j == nj - 1)
        def _():
            for kk in range(slabs):
                ys_ref[pl.ds(kk, tm, stride=slabs), :] = acc[:, kk * LANES:(kk + 1) * LANES]

    @pl.when((nv == 0) & (j == 0))
    def _():
        ys_ref[...] = jnp.zeros_like(ys_ref)


def _experts(tile_expert, tile_valid, slot_tok, h_slab, w_gate_up, b_gate_up, w_down, b_down, *, tile_rows,
             ff_block):
    n_tiles = tile_expert.shape[0]
    n_exp, d, two_f = w_gate_up.shape
    f = two_f // 2
    slabs = d // LANES
    nj = f // ff_block
    jj = lambda i, j, te, nv: jnp.where(nv[i] > 0, j, 0)
    grid_spec = pltpu.PrefetchScalarGridSpec(
        num_scalar_prefetch=2,
        grid=(n_tiles, nj),
        in_specs=[
            pl.BlockSpec((1, 1, tile_rows), lambda i, j, te, nv: (i, 0, 0)),
            pl.BlockSpec(memory_space=pl.ANY),
            pl.BlockSpec((1, d, ff_block), lambda i, j, te, nv: (te[i], 0, jj(i, j, te, nv))),
            pl.BlockSpec((1, d, ff_block), lambda i, j, te, nv: (te[i], 0, nj + jj(i, j, te, nv))),
            pl.BlockSpec((1, 1, ff_block), lambda i, j, te, nv: (te[i], 0, jj(i, j, te, nv))),
            pl.BlockSpec((1, 1, ff_block), lambda i, j, te, nv: (te[i], 0, nj + jj(i, j, te, nv))),
            pl.BlockSpec((1, ff_block, d), lambda i, j, te, nv: (te[i], jj(i, j, te, nv), 0)),
            pl.BlockSpec((1, 1, d), lambda i, j, te, nv: (te[i], 0, 0)),
        ],
        out_specs=pl.BlockSpec((tile_rows * slabs, LANES), lambda i, j, te, nv: (i, 0)),
        scratch_shapes=[
            pltpu.SMEM((tile_rows,), jnp.int32),
            pltpu.VMEM((tile_rows * slabs, LANES), F32),
            pltpu.VMEM((tile_rows, d), BF16),
            pltpu.VMEM((tile_rows, d), F32),
            pltpu.SemaphoreType.DMA,
            pltpu.SemaphoreType.DMA,
        ],
    )
    return pl.pallas_call(
        _experts_kernel,
        grid_spec=grid_spec,
        out_shape=jax.ShapeDtypeStruct((n_tiles * tile_rows * slabs, LANES), F32),
        compiler_params=_params(("arbitrary", "arbitrary")),
        name="moe_experts",
    )(tile_expert, tile_valid, slot_tok, h_slab, w_gate_up, w_gate_up, b_gate_up, b_gate_up, w_down, b_down)


def _combine_kernel(dest_ref, gate_ref, ys_hbm, h_ref, g2_ref, b2_ref, os_ref, op_ref, idx_smem, ybuf, fslab, fstd,
                    sem_idx, sem_rows, *, n_sample_blocks, alpha):
    i = pl.program_id(0)
    tb, d = h_ref.shape
    slabs = d // LANES

    cp = pltpu.make_async_copy(dest_ref.at[0, 0], idx_smem, sem_idx)
    cp.start()
    cp.wait()

    def issue(r, c):
        for k in range(TOP_K):
            _row_copy(ys_hbm, ybuf.at[k], sem_rows, idx_smem[r * TOP_K + k], r, slabs).start()
        return c

    lax.fori_loop(0, tb, issue, 0)

    def drain(r, c):
        for k in range(TOP_K):
            _row_copy(ys_hbm, ybuf.at[k], sem_rows, 0, r, slabs).wait()
        return c

    lax.fori_loop(0, tb, drain, 0)

    gates = gate_ref[...]
    f = gates[:, 0:1] * ybuf[0]
    for k in range(1, TOP_K):
        f = f + gates[:, k:k + 1] * ybuf[k]
    fslab[...] = f
    for kk in range(slabs):
        fstd[:, kk * LANES:(kk + 1) * LANES] = fslab[pl.ds(kk, tb, stride=slabs), :]
    y = _layer_norm(alpha * h_ref[...] + fstd[...], g2_ref[...], b2_ref[...])

    @pl.when(i < n_sample_blocks)
    def _():
        os_ref[...] = y

    @pl.when(i >= n_sample_blocks)
    def _():
        op_ref[...] = y


def _combine(dest, gates_slab, y_slab, h, g2, b2, *, n_sample, alpha, block):
    n, d = h.shape
    slabs = d // LANES
    nsb = n_sample // block
    kern = functools.partial(_combine_kernel, n_sample_blocks=nsb, alpha=alpha)
    const = lambda i: (0, 0)
    return pl.pallas_call(
        kern,
        grid=(n // block,),
        in_specs=[
            pl.BlockSpec((1, 1, block * TOP_K), lambda i: (i, 0, 0)),
            pl.BlockSpec((block * slabs, TOP_K), lambda i: (i, 0)),
            pl.BlockSpec(memory_space=pl.ANY),
            pl.BlockSpec((block, d), lambda i: (i, 0)),
            pl.BlockSpec((1, d), const),
            pl.BlockSpec((1, d), const),
        ],
        out_specs=[
            pl.BlockSpec((block, d), lambda i: (jnp.minimum(i, nsb - 1), 0)),
            pl.BlockSpec((block, d), lambda i: (jnp.maximum(i - nsb, 0), 0)),
        ],
        out_shape=[
            jax.ShapeDtypeStruct((n_sample, d), F32),
            jax.ShapeDtypeStruct((n - n_sample, d), F32),
        ],
        scratch_shapes=[
            pltpu.SMEM((block * TOP_K,), jnp.int32),
            pltpu.VMEM((TOP_K, block * slabs, LANES), F32),
            pltpu.VMEM((block * slabs, LANES), F32),
            pltpu.VMEM((block, d), F32),
            pltpu.SemaphoreType.DMA,
            pltpu.SemaphoreType.DMA,
        ],
        compiler_params=_params(("arbitrary",)),
        name="combine_norm",
    )(dest, gates_slab, y_slab, h, g2, b2)


def _tiles(n_tokens, d_ff):
    return dict(
        token_block=min(256, n_tokens),
        route_block=min(512, n_tokens),
        expert_rows=min(512, n_tokens),
        ff_block=min(256, d_ff),
    )


def kernel(x_prompt, x_sample, ln_in_g, ln_in_b, w_in, rpb, w_pool, pool_scale, w_out, ln1_g, ln1_b, w_router,
           b_router, w_gate_up, b_gate_up, w_down, b_down, ln2_g, ln2_b):
    depth = w_in.shape[0]
    assert depth == 1, "single-layer trunk"
    d = x_prompt.shape[-1]
    prompt_len = x_prompt.shape[1]
    sample_len = x_sample.shape[1]
    xs = x_sample.reshape(-1, d)
    xp = x_prompt.reshape(-1, d)
    ns, np_ = xs.shape[0], xp.shape[0]
    n = ns + np_
    assert x_prompt.shape[0] == 1 and ns % prompt_len == 0
    n_heads = rpb.shape[1]
    attn_width = n_heads * HEAD_DIM
    n_experts = w_router.shape[-1]
    d_ff = w_down.shape[2]
    alpha = float((2 * depth) ** 0.25)
    t = _tiles(n, d_ff)
    tb = t["token_block"]
    slabs = d // LANES
    row = lambda v: v.reshape(1, -1).astype(F32)

    q, k, v, u = _in_proj(xs, xp, row(ln_in_g), row(ln_in_b), w_in[0].astype(BF16), attn_width=attn_width, block=tb)

    bias = _attn_bias_table(rpb[0])
    attn_s = _attention(q, k, v, bias, seq_len=sample_len, n_seq=ns // sample_len, first_block=0)
    attn_p = _attention(q, k, v, bias, seq_len=prompt_len, n_seq=1, first_block=ns // prompt_len)

    wr = jnp.zeros((d, LANES), F32).at[:, :n_experts].set(w_router[0].astype(F32))
    wr_hi = wr.astype(BF16)
    wr_lo = (wr - wr_hi.astype(F32)).astype(BF16)
    br = jnp.zeros((1, LANES), F32).at[0, :n_experts].set(b_router[0].astype(F32))
    h, h_slab, logits = _mix(attn_s, attn_p, u, xs, xp, row(ln_in_g), row(ln_in_b), w_pool[0].astype(BF16),
                             row(pool_scale[0]), w_out[0].astype(BF16), row(ln1_g[0]), row(ln1_b[0]), wr_hi, wr_lo,
                             br, sample_len=sample_len, prompt_len=prompt_len, alpha=alpha, block=tb)

    meta, gates_pad, counts_pad = _route(logits, n_experts=n_experts, block=t["route_block"])
    top_idx = meta[:, :TOP_K]
    rank = meta[:, TOP_K:2 * TOP_K]
    gates = gates_pad[:, :TOP_K]
    counts = counts_pad[0, :n_experts]
    tm = t["expert_rows"]
    padded = (counts + tm - 1) // tm * tm
    pend = jnp.cumsum(padded)
    poff = pend - padded
    dest = poff[top_idx] + rank
    n_tiles = (n * TOP_K) // tm + n_experts
    tok_of_pair = jnp.repeat(jnp.arange(n, dtype=jnp.int32), TOP_K)
    slot_tok = jnp.zeros((n_tiles * tm,), jnp.int32).at[dest.reshape(-1)].set(tok_of_pair)
    tile_start = jnp.arange(n_tiles, dtype=jnp.int32) * tm
    tile_expert = jnp.minimum(jnp.sum(tile_start[:, None] >= pend[None, :], axis=1), n_experts - 1).astype(jnp.int32)
    tile_valid = jnp.clip((poff + counts)[tile_expert] - tile_start, 0, tm).astype(jnp.int32)

    y_slab = _experts(tile_expert, tile_valid, slot_tok.reshape(n_tiles, 1, tm), h_slab, w_gate_up[0],
                      b_gate_up[0].reshape(n_experts, 1, -1), w_down[0], b_down[0].reshape(n_experts, 1, -1),
                      tile_rows=tm, ff_block=t["ff_block"])

    gates_slab = jnp.repeat(gates, slabs, axis=0)
    out_s, out_p = _combine(dest.reshape(n // tb, 1, tb * TOP_K).astype(jnp.int32), gates_slab, y_slab, h,
                            row(ln2_g[0]), row(ln2_b[0]), n_sample=ns, alpha=alpha, block=tb)
    return out_p.reshape(x_prompt.shape), out_s.reshape(x_sample.shape)
```

```python
import functools

import jax
import jax.numpy as jnp
from jax import lax
from jax.experimental import pallas as pl
from jax.experimental.pallas import tpu as pltpu

F32 = jnp.float32
BF16 = jnp.bfloat16

LANES = 128
VMEM_LIMIT_BYTES = 60 * 1024 * 1024

GRID_W = 64
HEAD_DIM = 32
WIN_ROWS = 8
WIN_COLS = 16
POOL_WINDOWS = (2, 4, 8, 16)
TOP_K = 4
SWIGLU_LIMIT = 7.0
SWIGLU_ALPHA = 1.702
LN_EPS = 1e-5
NEG_BIG = -1e30

HEADS_PER_GROUP = LANES // HEAD_DIM
POOL_HALO = 16
ROW_DMA_UNROLL = 8
ATTN_ROW_UNROLL = 4
WEIGHT_STAGES = 2


def _layer_norm(x, g, b):
    mu = jnp.mean(x, axis=-1, keepdims=True)
    xc = x - mu
    var = jnp.mean(xc * xc, axis=-1, keepdims=True)
    return xc * lax.rsqrt(var + LN_EPS) * g + b


def _params(semantics):
    return pltpu.CompilerParams(dimension_semantics=semantics, vmem_limit_bytes=VMEM_LIMIT_BYTES)


def _in_proj_kernel(xs_ref, xp_ref, g_ref, b_ref, w_ref, q_ref, k_ref, v_ref, u_ref, *, n_sample_blocks,
                    attn_width, q_scale):
    i = pl.program_id(0)
    x = jnp.where(i < n_sample_blocks, xs_ref[...], xp_ref[...])
    h = _layer_norm(x, g_ref[...], b_ref[...])
    proj = jnp.dot(h.astype(BF16), w_ref[...], preferred_element_type=F32)
    a = attn_width
    q_ref[...] = (proj[:, :a] * q_scale).astype(BF16)
    k_ref[...] = proj[:, a:2 * a].astype(BF16)
    v_ref[...] = proj[:, 2 * a:3 * a].astype(BF16)
    u_ref[...] = proj[:, 3 * a:].astype(BF16)


def _in_proj(xs, xp, ln_g, ln_b, w_in_bf16, *, attn_width, block):
    ns, d = xs.shape
    np_ = xp.shape[0]
    n = ns + np_
    in_width = w_in_bf16.shape[1]
    pool_width = in_width - 3 * attn_width
    nsb = ns // block
    kern = functools.partial(_in_proj_kernel, n_sample_blocks=nsb, attn_width=attn_width,
                             q_scale=HEAD_DIM ** -0.5)
    row = lambda i: (i, 0)
    const = lambda i: (0, 0)
    return pl.pallas_call(
        kern,
        grid=(n // block,),
        in_specs=[
            pl.BlockSpec((block, d), lambda i: (jnp.minimum(i, nsb - 1), 0)),
            pl.BlockSpec((block, d), lambda i: (jnp.maximum(i - nsb, 0), 0)),
            pl.BlockSpec((1, d), const),
            pl.BlockSpec((1, d), const),
            pl.BlockSpec((d, in_width), const),
        ],
        out_specs=[
            pl.BlockSpec((block, attn_width), row),
            pl.BlockSpec((block, attn_width), row),
            pl.BlockSpec((block, attn_width), row),
            pl.BlockSpec((block, pool_width), row),
        ],
        out_shape=[
            jax.ShapeDtypeStruct((n, attn_width), BF16),
            jax.ShapeDtypeStruct((n, attn_width), BF16),
            jax.ShapeDtypeStruct((n, attn_width), BF16),
            jax.ShapeDtypeStruct((n, pool_width), BF16),
        ],
        compiler_params=_params(("arbitrary",)),
        name="in_proj",
    )(xs, xp, ln_g, ln_b, w_in_bf16)


def _attn_bias_table(rpb):
    n_heads = rpb.shape[0]
    cols = jnp.arange(GRID_W)
    col_start = jnp.clip(cols - WIN_COLS // 2, 0, GRID_W - WIN_COLS)
    in_win = (cols[None, :] >= col_start[:, None]) & (cols[None, :] < col_start[:, None] + WIN_COLS)
    dc = cols[None, :] - cols[:, None] + (WIN_COLS - 1)
    select = (dc[None] == jnp.arange(2 * WIN_COLS - 1)[:, None, None]).astype(F32)
    toep = jnp.einsum('hrd,dqk->hrqk', rpb.astype(F32), select, precision=lax.Precision.HIGHEST)
    toep = jnp.where(in_win[None, None], toep, NEG_BIG)
    t = jnp.stack([toep[:, WIN_ROWS - 1 - c:2 * WIN_ROWS - 1 - c] for c in range(WIN_ROWS)], axis=1)
    t = jnp.transpose(t, (0, 1, 3, 2, 4))
    g = n_heads // HEADS_PER_GROUP
    t = t.reshape(g, HEADS_PER_GROUP, WIN_ROWS, GRID_W, WIN_ROWS * GRID_W)
    t = jnp.transpose(t, (0, 2, 1, 3, 4))
    return t.reshape(g, WIN_ROWS, HEADS_PER_GROUP * GRID_W, WIN_ROWS * GRID_W)


def _attention_kernel(q_ref, k_ref, v_ref, bias_ref, o_ref, *, rows):
    lane_head = lax.broadcasted_iota(jnp.int32, (1, LANES), 1) // HEAD_DIM
    kv_rows = WIN_ROWS * GRID_W

    def body(r, carry):
        rs = jnp.clip(r - WIN_ROWS // 2, 0, rows - WIN_ROWS)
        case = jnp.minimum(r, WIN_ROWS // 2) + jnp.maximum(r - (rows - WIN_ROWS // 2), 0)
        q = q_ref[pl.ds(pl.multiple_of(r * GRID_W, GRID_W), GRID_W), :]
        kb = k_ref[pl.ds(pl.multiple_of(rs * GRID_W, GRID_W), kv_rows), :]
        vb = v_ref[pl.ds(pl.multiple_of(rs * GRID_W, GRID_W), kv_rows), :]
        zero = jnp.zeros_like(q)
        qs = jnp.concatenate([jnp.where(lane_head == h, q, zero) for h in range(HEADS_PER_GROUP)], axis=0)
        s = lax.dot_general(qs, kb, (((1,), (1,)), ((), ())), preferred_element_type=F32)
        s = s + bias_ref[0, case]
        m = jnp.max(s, axis=-1, keepdims=True)
        p = jnp.exp(s - m)
        l = jnp.sum(p, axis=-1, keepdims=True)
        o = jnp.dot(p.astype(BF16), vb, preferred_element_type=F32) / l
        out = jnp.zeros((GRID_W, LANES), F32)
        for h in range(HEADS_PER_GROUP):
            out = jnp.where(lane_head == h, o[h * GRID_W:(h + 1) * GRID_W], out)
        o_ref[pl.ds(pl.multiple_of(r * GRID_W, GRID_W), GRID_W), :] = out.astype(o_ref.dtype)
        return carry

    lax.fori_loop(0, rows, body, 0, unroll=ATTN_ROW_UNROLL)


def _attention(q, k, v, bias, *, seq_len, n_seq, first_block):
    a = q.shape[1]
    groups = a // LANES
    rows = seq_len // GRID_W
    assert rows >= WIN_ROWS and seq_len % GRID_W == 0 and rows % ATTN_ROW_UNROLL == 0
    kern = functools.partial(_attention_kernel, rows=rows)
    tok = lambda b, g: (first_block + b, g)
    return pl.pallas_call(
        kern,
        grid=(n_seq, groups),
        in_specs=[
            pl.BlockSpec((seq_len, LANES), tok),
            pl.BlockSpec((seq_len, LANES), tok),
            pl.BlockSpec((seq_len, LANES), tok),
            pl.BlockSpec((1,) + bias.shape[1:], lambda b, g: (g, 0, 0, 0)),
        ],
        out_specs=pl.BlockSpec((seq_len, LANES), lambda b, g: (b, g)),
        out_shape=jax.ShapeDtypeStruct((n_seq * seq_len, a), BF16),
        compiler_params=_params(("arbitrary", "arbitrary")),
        name="nbr_attention",
    )(q, k, v, bias)


def _mix_kernel(attn_s_ref, attn_p_ref, u_ref, utop_ref, ubot_ref, xs_ref, xp_ref, lng_ref, lnb_ref, wpool_ref,
                pscale_ref, wout_ref, g1_ref, b1_ref, wrh_ref, wrl_ref, br_ref,
                h_ref, hs_ref, logit_ref, ubuf, *, n_sample_blocks, sample_len, n_sample_tokens, prompt_len,
                alpha):
    i = pl.program_id(0)
    tb = u_ref.shape[0]
    d = h_ref.shape[1]
    a = attn_s_ref.shape[1]
    pg = wpool_ref.shape[1]
    is_sample = i < n_sample_blocks
    tok0 = i * tb
    seq_len = jnp.where(is_sample, sample_len, prompt_len)
    local0 = jnp.where(is_sample, lax.rem(tok0, sample_len), tok0 - n_sample_tokens)
    top_ok = local0 > 0
    bot_ok = local0 + tb < seq_len

    ubuf[0:POOL_HALO, :] = jnp.where(top_ok, utop_ref[...].astype(F32), 0.0)
    ubuf[POOL_HALO:POOL_HALO + tb, :] = u_ref[...].astype(F32)
    ubuf[POOL_HALO + tb:, :] = jnp.where(bot_ok, ubot_ref[...].astype(F32), 0.0)
    pos = local0 + lax.broadcasted_iota(jnp.int32, (tb, 1), 0)

    attn = jnp.where(is_sample, attn_s_ref[...], attn_p_ref[...])
    mix = jnp.dot(attn, wout_ref[0:a, :], preferred_element_type=F32)
    for g, w in enumerate(POOL_WINDOWS):
        half = w // 2
        c0 = g * pg
        win = ubuf[POOL_HALO - half:POOL_HALO - half + tb, c0:c0 + pg]
        for dlt in range(-half + 1, half):
            win = win + ubuf[POOL_HALO + dlt:POOL_HALO + dlt + tb, c0:c0 + pg]
        cnt = (jnp.minimum(pos + half, seq_len) - jnp.maximum(pos - half, 0)).astype(F32)
        pooled = win / cnt - ubuf[POOL_HALO:POOL_HALO + tb, c0:c0 + pg]
        mixed = jnp.dot(pooled.astype(BF16), wpool_ref[g], preferred_element_type=F32)
        mixed = mixed * pscale_ref[:, c0:c0 + pg]
        mix = mix + jnp.dot(mixed.astype(BF16), wout_ref[a + c0:a + c0 + pg, :], preferred_element_type=F32)

    x = jnp.where(is_sample, xs_ref[...], xp_ref[...])
    h0 = _layer_norm(x, lng_ref[...], lnb_ref[...])
    h = _layer_norm(alpha * h0 + mix, g1_ref[...], b1_ref[...])
    h_ref[...] = h
    slabs = d // LANES
    for kk in range(slabs):
        hs_ref[pl.ds(kk, tb, stride=slabs), :] = h[:, kk * LANES:(kk + 1) * LANES]
    hi = h.astype(BF16)
    lo = (h - hi.astype(F32)).astype(BF16)
    logits = jnp.dot(hi, wrh_ref[...], preferred_element_type=F32)
    logits = logits + (jnp.dot(lo, wrh_ref[...], preferred_element_type=F32)
                       + jnp.dot(hi, wrl_ref[...], preferred_element_type=F32))
    logit_ref[...] = logits + br_ref[...]


def _mix(attn_s, attn_p, u, xs, xp, ln_g, ln_b, w_pool_bf16, pool_scale, w_out_bf16, g1, b1, wr_hi, wr_lo, br,
         *, sample_len, prompt_len, alpha, block):
    ns, d = xs.shape
    np_ = xp.shape[0]
    n = ns + np_
    a = attn_s.shape[1]
    p = u.shape[1]
    nsb = ns // block
    slabs = d // LANES
    halo_blocks = block // POOL_HALO
    n_halo = n // POOL_HALO
    kern = functools.partial(_mix_kernel, n_sample_blocks=nsb, sample_len=sample_len, n_sample_tokens=ns,
                             prompt_len=prompt_len, alpha=alpha)
    row = lambda i: (i, 0)
    const = lambda i: (0, 0)
    srow = lambda i: (jnp.minimum(i, nsb - 1), 0)
    prow = lambda i: (jnp.maximum(i - nsb, 0), 0)
    return pl.pallas_call(
        kern,
        grid=(n // block,),
        in_specs=[
            pl.BlockSpec((block, a), srow),
            pl.BlockSpec((block, a), prow),
            pl.BlockSpec((block, p), row),
            pl.BlockSpec((POOL_HALO, p), lambda i: (jnp.maximum(i * halo_blocks - 1, 0), 0)),
            pl.BlockSpec((POOL_HALO, p), lambda i: (jnp.minimum((i + 1) * halo_blocks, n_halo - 1), 0)),
            pl.BlockSpec((block, d), srow),
            pl.BlockSpec((block, d), prow),
            pl.BlockSpec((1, d), const),
            pl.BlockSpec((1, d), const),
            pl.BlockSpec(w_pool_bf16.shape, lambda i: (0, 0, 0)),
            pl.BlockSpec((1, p), const),
            pl.BlockSpec(w_out_bf16.shape, const),
            pl.BlockSpec((1, d), const),
            pl.BlockSpec((1, d), const),
            pl.BlockSpec(wr_hi.shape, const),
            pl.BlockSpec(wr_lo.shape, const),
            pl.BlockSpec((1, LANES), const),
        ],
        out_specs=[
            pl.BlockSpec((block, d), row),
            pl.BlockSpec((block * slabs, LANES), row),
            pl.BlockSpec((block, LANES), row),
        ],
        out_shape=[
            jax.ShapeDtypeStruct((n, d), F32),
            jax.ShapeDtypeStruct((n * slabs, LANES), F32),
            jax.ShapeDtypeStruct((n, LANES), F32),
        ],
        scratch_shapes=[pltpu.VMEM((block + 2 * POOL_HALO, p), F32)],
        compiler_params=_params(("arbitrary",)),
        name="mix_norm_logits",
    )(attn_s, attn_p, u, u, u, xs, xp, ln_g, ln_b, w_pool_bf16, pool_scale, w_out_bf16, g1, b1, wr_hi, wr_lo, br)


def _route_kernel(logit_ref, meta_ref, gate_ref, count_ref, carry, *, n_experts):
    i = pl.program_id(0)
    tb = logit_ref.shape[0]

    @pl.when(i == 0)
    def _():
        carry[...] = jnp.zeros_like(carry)

    lane = lax.broadcasted_iota(jnp.int32, (tb, LANES), 1)
    lane_f = lane.astype(F32)
    cur = jnp.where(lane < n_experts, logit_ref[...], -jnp.inf)
    vals, idxs, hots = [], [], []
    for _ in range(TOP_K):
        m = jnp.max(cur, axis=-1, keepdims=True)
        idx = jnp.min(jnp.where(cur == m, lane_f, float(LANES)), axis=-1, keepdims=True)
        hot = lane_f == idx
        cur = jnp.where(hot, -jnp.inf, cur)
        vals.append(m)
        idxs.append(idx)
        hots.append(hot)
    exps = [jnp.exp(v - vals[0]) for v in vals]
    den = exps[0]
    for e in exps[1:]:
        den = den + e
    chosen = jnp.zeros((tb, LANES), F32)
    for hot in hots:
        chosen = jnp.where(hot, 1.0, chosen)
    r_io = lax.broadcasted_iota(jnp.int32, (tb, tb), 0)
    c_io = lax.broadcasted_iota(jnp.int32, (tb, tb), 1)
    tri = jnp.where(r_io > c_io, 1.0, 0.0).astype(BF16)
    before = jnp.dot(tri, chosen.astype(BF16), preferred_element_type=F32) + carry[...]
    meta = jnp.zeros((tb, LANES), F32)
    gates = jnp.zeros((tb, LANES), F32)
    for k in range(TOP_K):
        rank = jnp.sum(jnp.where(hots[k], before, 0.0), axis=-1, keepdims=True)
        meta = jnp.where(lane == k, idxs[k], meta)
        meta = jnp.where(lane == TOP_K + k, rank, meta)
        gates = jnp.where(lane == k, exps[k] / den, gates)
    meta_ref[...] = meta.astype(jnp.int32)
    gate_ref[...] = gates
    total = carry[...] + jnp.sum(chosen, axis=0, keepdims=True)
    carry[...] = total
    count_ref[...] = total.astype(jnp.int32)


def _route(logits, *, n_experts, block):
    n = logits.shape[0]
    assert n * TOP_K < 2 ** 24
    row = lambda i: (i, 0)
    return pl.pallas_call(
        functools.partial(_route_kernel, n_experts=n_experts),
        grid=(n // block,),
        in_specs=[pl.BlockSpec((block, LANES), row)],
        out_specs=[
            pl.BlockSpec((block, LANES), row),
            pl.BlockSpec((block, LANES), row),
            pl.BlockSpec((1, LANES), lambda i: (0, 0)),
        ],
        out_shape=[
            jax.ShapeDtypeStruct((n, LANES), jnp.int32),
            jax.ShapeDtypeStruct((n, LANES), F32),
            jax.ShapeDtypeStruct((1, LANES), jnp.int32),
        ],
        scratch_shapes=[pltpu.VMEM((1, LANES), F32)],
        compiler_params=_params(("arbitrary",)),
        name="route_topk",
    )(logits)


def _row_copy(src_hbm, dst_vmem, sem, src_row, dst_row, slabs):
    return pltpu.make_async_copy(
        src_hbm.at[pl.ds(pl.multiple_of(src_row * slabs, slabs), slabs), :],
        dst_vmem.at[pl.ds(pl.multiple_of(dst_row * slabs, slabs), slabs), :],
        sem)


def _gather_rows(src_hbm, dst_vmem, sem, idx_smem, idx_base, idx_stride, n_rows, slabs):
    def issue(b, c):
        for s in range(ROW_DMA_UNROLL):
            r = b * ROW_DMA_UNROLL + s
            _row_copy(src_hbm, dst_vmem, sem, idx_smem[idx_base + r * idx_stride], r, slabs).start()
        return c

    lax.fori_loop(0, n_rows // ROW_DMA_UNROLL, issue, 0)

    def drain(b, c):
        for s in range(ROW_DMA_UNROLL):
            _row_copy(src_hbm, dst_vmem, sem, 0, b * ROW_DMA_UNROLL + s, slabs).wait()
        return c

    lax.fori_loop(0, n_rows // ROW_DMA_UNROLL, drain, 0)


def _experts_kernel(te_ref, nv_ref, first_ref, nxt_ref, tok_ref, hs_hbm, wgu_hbm, wd_hbm, bgu_ref, bd_ref, ys_ref,
                    idx_smem, xb16, acc, wg16, wu16, wd16, stg_g, stg_u, stg_d, sem_idx, sem_rows, sem_w):
    i = pl.program_id(0)
    nv = nv_ref[i]
    tm, d = xb16.shape
    slabs = d // LANES
    n_chunks, _, fc = wg16.shape
    d_ff = n_chunks * fc

    def weight_copies(e, c, slot):
        col = pl.multiple_of(c * fc, fc)
        return (
            pltpu.make_async_copy(wgu_hbm.at[e, :, pl.ds(col, fc)], stg_g.at[slot], sem_w.at[slot]),
            pltpu.make_async_copy(wgu_hbm.at[e, :, pl.ds(d_ff + col, fc)], stg_u.at[slot], sem_w.at[slot]),
            pltpu.make_async_copy(wd_hbm.at[e, pl.ds(col, fc), :], stg_d.at[slot], sem_w.at[slot]),
        )

    def start_chunk(e, c, slot):
        for cp in weight_copies(e, c, slot):
            cp.start()

    @pl.when(nv > 0)
    def _():
        e = te_ref[i]
        first = first_ref[i] > 0

        @pl.when(first & (i == 0))
        def _():
            for c in range(WEIGHT_STAGES):
                start_chunk(e, c, c)

        cp = pltpu.make_async_copy(tok_ref.at[0, 0], idx_smem, sem_idx)
        cp.start()
        cp.wait()
        _gather_rows(hs_hbm, ys_ref, sem_rows, idx_smem, 0, 1, tm, slabs)
        for kk in range(slabs):
            xb16[:, kk * LANES:(kk + 1) * LANES] = ys_ref[pl.ds(kk, tm, stride=slabs), :].astype(BF16)
        acc[...] = jnp.broadcast_to(bd_ref[0], (tm, d))

        def chunk(c, carry):
            slot = lax.rem(c, WEIGHT_STAGES)

            @pl.when(first)
            def _():
                for cp_w in weight_copies(e, c, slot):
                    cp_w.wait()
                wg16[c] = stg_g[slot].astype(BF16)
                wu16[c] = stg_u[slot].astype(BF16)
                wd16[c] = stg_d[slot].astype(BF16)

                @pl.when(c + WEIGHT_STAGES < n_chunks)
                def _():
                    start_chunk(e, c + WEIGHT_STAGES, slot)

            x = xb16[...]
            gate = jnp.dot(x, wg16[c], preferred_element_type=F32) + bgu_ref[0, pl.ds(c, 1), :]
            up = jnp.dot(x, wu16[c], preferred_element_type=F32) + bgu_ref[0, pl.ds(n_chunks + c, 1), :]
            gate = jnp.minimum(gate, SWIGLU_LIMIT)
            up = jnp.clip(up, -SWIGLU_LIMIT, SWIGLU_LIMIT)
            act = (up + 1.0) * (gate * jax.nn.sigmoid(SWIGLU_ALPHA * gate))
            acc[...] += jnp.dot(act.astype(BF16), wd16[c], preferred_element_type=F32)
            return carry

        lax.fori_loop(0, n_chunks, chunk, 0)

        nxt = nxt_ref[i]

        @pl.when(nxt >= 0)
        def _():
            for c in range(WEIGHT_STAGES):
                start_chunk(nxt, c, c)

        for kk in range(slabs):
            ys_ref[pl.ds(kk, tm, stride=slabs), :] = acc[:, kk * LANES:(kk + 1) * LANES]

    @pl.when(nv == 0)
    def _():
        ys_ref[...] = jnp.zeros_like(ys_ref)


def _experts(tile_expert, tile_valid, tile_first, tile_next, slot_tok, h_slab, w_gate_up, b_gate_up, w_down, b_down,
             *, tile_rows, ff_chunk):
    n_tiles = tile_expert.shape[0]
    n_exp, d, two_f = w_gate_up.shape
    f = two_f // 2
    slabs = d // LANES
    n_chunks = f // ff_chunk
    assert n_chunks >= WEIGHT_STAGES and tile_rows % ROW_DMA_UNROLL == 0
    grid_spec = pltpu.PrefetchScalarGridSpec(
        num_scalar_prefetch=4,
        grid=(n_tiles,),
        in_specs=[
            pl.BlockSpec((1, 1, tile_rows), lambda i, te, nv, fs, nx: (i, 0, 0)),
            pl.BlockSpec(memory_space=pl.ANY),
            pl.BlockSpec(memory_space=pl.ANY),
            pl.BlockSpec(memory_space=pl.ANY),
            pl.BlockSpec((1, 2 * n_chunks, ff_chunk), lambda i, te, nv, fs, nx: (te[i], 0, 0)),
            pl.BlockSpec((1, 1, d), lambda i, te, nv, fs, nx: (te[i], 0, 0)),
        ],
        out_specs=pl.BlockSpec((tile_rows * slabs, LANES), lambda i, te, nv, fs, nx: (i, 0)),
        scratch_shapes=[
            pltpu.SMEM((tile_rows,), jnp.int32),
            pltpu.VMEM((tile_rows, d), BF16),
            pltpu.VMEM((tile_rows, d), F32),
            pltpu.VMEM((n_chunks, d, ff_chunk), BF16),
            pltpu.VMEM((n_chunks, d, ff_chunk), BF16),
            pltpu.VMEM((n_chunks, ff_chunk, d), BF16),
            pltpu.VMEM((WEIGHT_STAGES, d, ff_chunk), F32),
            pltpu.VMEM((WEIGHT_STAGES, d, ff_chunk), F32),
            pltpu.VMEM((WEIGHT_STAGES, ff_chunk, d), F32),
            pltpu.SemaphoreType.DMA,
            pltpu.SemaphoreType.DMA,
            pltpu.SemaphoreType.DMA((WEIGHT_STAGES,)),
        ],
    )
    return pl.pallas_call(
        _experts_kernel,
        grid_spec=grid_spec,
        out_shape=jax.ShapeDtypeStruct((n_tiles * tile_rows * slabs, LANES), F32),
        compiler_params=_params(("arbitrary",)),
        name="moe_experts",
    )(tile_expert, tile_valid, tile_first, tile_next, slot_tok, h_slab, w_gate_up, w_down, b_gate_up, b_down)


def _combine_kernel(dest_ref, gate_ref, ys_hbm, h_ref, g2_ref, b2_ref, os_ref, op_ref, idx_smem, ybuf, fslab, fstd,
                    sem_idx, sem_rows, *, n_sample_blocks, alpha):
    i = pl.program_id(0)
    tb, d = h_ref.shape
    slabs = d // LANES

    cp = pltpu.make_async_copy(dest_ref.at[0, 0], idx_smem, sem_idx)
    cp.start()
    cp.wait()
    for k in range(TOP_K):
        _gather_rows(ys_hbm, ybuf.at[k], sem_rows.at[k], idx_smem, k, TOP_K, tb, slabs)

    gates = gate_ref[...]
    f = gates[:, 0:1] * ybuf[0]
    for k in range(1, TOP_K):
        f = f + gates[:, k:k + 1] * ybuf[k]
    fslab[...] = f
    for kk in range(slabs):
        fstd[:, kk * LANES:(kk + 1) * LANES] = fslab[pl.ds(kk, tb, stride=slabs), :]
    y = _layer_norm(alpha * h_ref[...] + fstd[...], g2_ref[...], b2_ref[...])

    @pl.when(i < n_sample_blocks)
    def _():
        os_ref[...] = y

    @pl.when(i >= n_sample_blocks)
    def _():
        op_ref[...] = y


def _combine(dest, gates_slab, y_slab, h, g2, b2, *, n_sample, alpha, block):
    n, d = h.shape
    slabs = d // LANES
    nsb = n_sample // block
    assert block % ROW_DMA_UNROLL == 0
    kern = functools.partial(_combine_kernel, n_sample_blocks=nsb, alpha=alpha)
    const = lambda i: (0, 0)
    return pl.pallas_call(
        kern,
        grid=(n // block,),
        in_specs=[
            pl.BlockSpec((1, 1, block * TOP_K), lambda i: (i, 0, 0)),
            pl.BlockSpec((block * slabs, TOP_K), lambda i: (i, 0)),
            pl.BlockSpec(memory_space=pl.ANY),
            pl.BlockSpec((block, d), lambda i: (i, 0)),
            pl.BlockSpec((1, d), const),
            pl.BlockSpec((1, d), const),
        ],
        out_specs=[
            pl.BlockSpec((block, d), lambda i: (jnp.minimum(i, nsb - 1), 0)),
            pl.BlockSpec((block, d), lambda i: (jnp.maximum(i - nsb, 0), 0)),
        ],
        out_shape=[
            jax.ShapeDtypeStruct((n_sample, d), F32),
            jax.ShapeDtypeStruct((n - n_sample, d), F32),
        ],
        scratch_shapes=[
            pltpu.SMEM((block * TOP_K,), jnp.int32),
            pltpu.VMEM((TOP_K, block * slabs, LANES), F32),
            pltpu.VMEM((block * slabs, LANES), F32),
            pltpu.VMEM((block, d), F32),
            pltpu.SemaphoreType.DMA,
            pltpu.SemaphoreType.DMA((TOP_K,)),
        ],
        compiler_params=_params(("arbitrary",)),
        name="combine_norm",
    )(dest, gates_slab, y_slab, h, g2, b2)


def _tiles(n_tokens, d_ff):
    return dict(
        token_block=min(256, n_tokens),
        route_block=min(512, n_tokens),
        expert_rows=min(512, n_tokens),
        ff_chunk=min(256, d_ff),
    )


def kernel(x_prompt, x_sample, ln_in_g, ln_in_b, w_in, rpb, w_pool, pool_scale, w_out, ln1_g, ln1_b, w_router,
           b_router, w_gate_up, b_gate_up, w_down, b_down, ln2_g, ln2_b):
    depth = w_in.shape[0]
    assert depth == 1, "single-layer trunk"
    d = x_prompt.shape[-1]
    prompt_len = x_prompt.shape[1]
    sample_len = x_sample.shape[1]
    xs = x_sample.reshape(-1, d)
    xp = x_prompt.reshape(-1, d)
    ns, np_ = xs.shape[0], xp.shape[0]
    n = ns + np_
    assert x_prompt.shape[0] == 1 and ns % prompt_len == 0
    n_heads = rpb.shape[1]
    attn_width = n_heads * HEAD_DIM
    n_experts = w_router.shape[-1]
    d_ff = w_down.shape[2]
    alpha = float((2 * depth) ** 0.25)
    t = _tiles(n, d_ff)
    tb = t["token_block"]
    slabs = d // LANES
    row = lambda v: v.reshape(1, -1).astype(F32)

    q, k, v, u = _in_proj(xs, xp, row(ln_in_g), row(ln_in_b), w_in[0].astype(BF16), attn_width=attn_width, block=tb)

    bias = _attn_bias_table(rpb[0])
    attn_s = _attention(q, k, v, bias, seq_len=sample_len, n_seq=ns // sample_len, first_block=0)
    attn_p = _attention(q, k, v, bias, seq_len=prompt_len, n_seq=1, first_block=ns // prompt_len)

    wr = jnp.pad(w_router[0].astype(F32), ((0, 0), (0, LANES - n_experts)))
    wr_hi = wr.astype(BF16)
    wr_lo = (wr - wr_hi.astype(F32)).astype(BF16)
    br = jnp.pad(b_router[0].astype(F32), (0, LANES - n_experts)).reshape(1, LANES)
    h, h_slab, logits = _mix(attn_s, attn_p, u, xs, xp, row(ln_in_g), row(ln_in_b), w_pool[0].astype(BF16),
                             row(pool_scale[0]), w_out[0].astype(BF16), row(ln1_g[0]), row(ln1_b[0]), wr_hi, wr_lo,
                             br, sample_len=sample_len, prompt_len=prompt_len, alpha=alpha, block=tb)

    meta, gates_pad, counts_pad = _route(logits, n_experts=n_experts, block=t["route_block"])
    top_idx = meta[:, :TOP_K]
    rank = meta[:, TOP_K:2 * TOP_K]
    gates = gates_pad[:, :TOP_K]
    counts = counts_pad[0, :n_experts]
    tm = t["expert_rows"]
    padded = (counts + tm - 1) // tm * tm
    pend = jnp.cumsum(padded)
    poff = pend - padded
    expert_ids = jnp.arange(n_experts, dtype=jnp.int32)
    base = jnp.sum(jnp.where(top_idx[:, :, None] == expert_ids, poff, 0), axis=-1)
    dest = (base + rank).astype(jnp.int32)
    n_tiles = (n * TOP_K) // tm + n_experts
    tok_of_pair = jnp.repeat(jnp.arange(n, dtype=jnp.int32), TOP_K)
    slot_tok = jnp.zeros((n_tiles * tm,), jnp.int32).at[dest.reshape(-1)].set(tok_of_pair)
    tile_start = jnp.arange(n_tiles, dtype=jnp.int32) * tm
    tile_expert = jnp.minimum(jnp.sum(tile_start[:, None] >= pend[None, :], axis=1), n_experts - 1).astype(jnp.int32)
    tile_valid = jnp.clip((poff + counts)[tile_expert] - tile_start, 0, tm).astype(jnp.int32)
    used = tile_valid > 0
    prev_expert = jnp.concatenate([jnp.full((1,), -1, jnp.int32), tile_expert[:-1]])
    tile_first = (used & (tile_expert != prev_expert)).astype(jnp.int32)
    next_first = jnp.concatenate([tile_first[1:], jnp.zeros((1,), jnp.int32)])
    next_expert = jnp.concatenate([tile_expert[1:], jnp.zeros((1,), jnp.int32)])
    tile_next = jnp.where(next_first > 0, next_expert, -1).astype(jnp.int32)

    y_slab = _experts(tile_expert, tile_valid, tile_first, tile_next, slot_tok.reshape(n_tiles, 1, tm), h_slab,
                      w_gate_up[0], b_gate_up[0].reshape(n_experts, -1, t["ff_chunk"]), w_down[0],
                      b_down[0].reshape(n_experts, 1, -1), tile_rows=tm, ff_chunk=t["ff_chunk"])

    gates_slab = jnp.repeat(gates, slabs, axis=0)
    out_s, out_p = _combine(dest.reshape(n // tb, 1, tb * TOP_K), gates_slab, y_slab, h,
                            row(ln2_g[0]), row(ln2_b[0]), n_sample=ns, alpha=alpha, block=tb)
    return out_p.reshape(x_prompt.shape), out_s.reshape(x_sample.shape)
```

```python
import functools

import jax
import jax.numpy as jnp
from jax import lax
from jax.experimental import pallas as pl
from jax.experimental.pallas import tpu as pltpu

F32 = jnp.float32
BF16 = jnp.bfloat16

LANES = 128
VMEM_LIMIT_BYTES = 60 * 1024 * 1024

GRID_W = 64
HEAD_DIM = 32
WIN_ROWS = 8
WIN_COLS = 16
POOL_WINDOWS = (2, 4, 8, 16)
TOP_K = 4
SWIGLU_LIMIT = 7.0
SWIGLU_ALPHA = 1.702
LN_EPS = 1e-5
NEG_BIG = -1e30

HEADS_PER_GROUP = LANES // HEAD_DIM
POOL_HALO = 16
ROW_DMA_UNROLL = 8
ATTN_ROW_UNROLL = 4
WEIGHT_STAGES = 2


def _layer_norm(x, g, b):
    mu = jnp.mean(x, axis=-1, keepdims=True)
    xc = x - mu
    var = jnp.mean(xc * xc, axis=-1, keepdims=True)
    return xc * lax.rsqrt(var + LN_EPS) * g + b


def _params(semantics):
    return pltpu.CompilerParams(dimension_semantics=semantics, vmem_limit_bytes=VMEM_LIMIT_BYTES)


def _in_proj_kernel(xs_ref, xp_ref, g_ref, b_ref, w_ref, q_ref, k_ref, v_ref, u_ref, *, n_sample_blocks,
                    attn_width, q_scale):
    i = pl.program_id(0)
    x = jnp.where(i < n_sample_blocks, xs_ref[...], xp_ref[...])
    h = _layer_norm(x, g_ref[...], b_ref[...])
    proj = jnp.dot(h.astype(BF16), w_ref[...], preferred_element_type=F32)
    a = attn_width
    q_ref[...] = (proj[:, :a] * q_scale).astype(BF16)
    k_ref[...] = proj[:, a:2 * a].astype(BF16)
    v_ref[...] = proj[:, 2 * a:3 * a].astype(BF16)
    u_ref[...] = proj[:, 3 * a:].astype(BF16)


def _in_proj(xs, xp, ln_g, ln_b, w_in_bf16, *, attn_width, block):
    ns, d = xs.shape
    np_ = xp.shape[0]
    n = ns + np_
    in_width = w_in_bf16.shape[1]
    pool_width = in_width - 3 * attn_width
    nsb = ns // block
    kern = functools.partial(_in_proj_kernel, n_sample_blocks=nsb, attn_width=attn_width,
                             q_scale=HEAD_DIM ** -0.5)
    row = lambda i: (i, 0)
    const = lambda i: (0, 0)
    return pl.pallas_call(
        kern,
        grid=(n // block,),
        in_specs=[
            pl.BlockSpec((block, d), lambda i: (jnp.minimum(i, nsb - 1), 0)),
            pl.BlockSpec((block, d), lambda i: (jnp.maximum(i - nsb, 0), 0)),
            pl.BlockSpec((1, d), const),
            pl.BlockSpec((1, d), const),
            pl.BlockSpec((d, in_width), const),
        ],
        out_specs=[
            pl.BlockSpec((block, attn_width), row),
            pl.BlockSpec((block, attn_width), row),
            pl.BlockSpec((block, attn_width), row),
            pl.BlockSpec((block, pool_width), row),
        ],
        out_shape=[
            jax.ShapeDtypeStruct((n, attn_width), BF16),
            jax.ShapeDtypeStruct((n, attn_width), BF16),
            jax.ShapeDtypeStruct((n, attn_width), BF16),
            jax.ShapeDtypeStruct((n, pool_width), BF16),
        ],
        compiler_params=_params(("arbitrary",)),
        name="in_proj",
    )(xs, xp, ln_g, ln_b, w_in_bf16)


def _attn_bias_table(rpb):
    n_heads = rpb.shape[0]
    cols = jnp.arange(GRID_W)
    col_start = jnp.clip(cols - WIN_COLS // 2, 0, GRID_W - WIN_COLS)
    in_win = (cols[None, :] >= col_start[:, None]) & (cols[None, :] < col_start[:, None] + WIN_COLS)
    dc = cols[None, :] - cols[:, None] + (WIN_COLS - 1)
    select = (dc[None] == jnp.arange(2 * WIN_COLS - 1)[:, None, None]).astype(F32)
    toep = jnp.einsum('hrd,dqk->hrqk', rpb.astype(F32), select, precision=lax.Precision.HIGHEST)
    toep = jnp.where(in_win[None, None], toep, NEG_BIG)
    t = jnp.stack([toep[:, WIN_ROWS - 1 - c:2 * WIN_ROWS - 1 - c] for c in range(WIN_ROWS)], axis=1)
    t = jnp.transpose(t, (0, 1, 3, 2, 4))
    g = n_heads // HEADS_PER_GROUP
    t = t.reshape(g, HEADS_PER_GROUP, WIN_ROWS, GRID_W, WIN_ROWS * GRID_W)
    t = jnp.transpose(t, (0, 2, 1, 3, 4))
    return t.reshape(g, WIN_ROWS, HEADS_PER_GROUP * GRID_W, WIN_ROWS * GRID_W)


def _attention_kernel(q_ref, k_ref, v_ref, bias_ref, o_ref, *, rows):
    lane_head = lax.broadcasted_iota(jnp.int32, (1, LANES), 1) // HEAD_DIM
    kv_rows = WIN_ROWS * GRID_W

    def body(r, carry):
        rs = jnp.clip(r - WIN_ROWS // 2, 0, rows - WIN_ROWS)
        case = jnp.minimum(r, WIN_ROWS // 2) + jnp.maximum(r - (rows - WIN_ROWS // 2), 0)
        q = q_ref[pl.ds(pl.multiple_of(r * GRID_W, GRID_W), GRID_W), :]
        kb = k_ref[pl.ds(pl.multiple_of(rs * GRID_W, GRID_W), kv_rows), :]
        vb = v_ref[pl.ds(pl.multiple_of(rs * GRID_W, GRID_W), kv_rows), :]
        zero = jnp.zeros_like(q)
        qs = jnp.concatenate([jnp.where(lane_head == h, q, zero) for h in range(HEADS_PER_GROUP)], axis=0)
        s = lax.dot_general(qs, kb, (((1,), (1,)), ((), ())), preferred_element_type=F32)
        s = s + bias_ref[0, case]
        m = jnp.max(s, axis=-1, keepdims=True)
        p = jnp.exp(s - m)
        l = jnp.sum(p, axis=-1, keepdims=True)
        o = jnp.dot(p.astype(BF16), vb, preferred_element_type=F32) / l
        out = jnp.zeros((GRID_W, LANES), F32)
        for h in range(HEADS_PER_GROUP):
            out = jnp.where(lane_head == h, o[h * GRID_W:(h + 1) * GRID_W], out)
        o_ref[pl.ds(pl.multiple_of(r * GRID_W, GRID_W), GRID_W), :] = out.astype(o_ref.dtype)
        return carry

    lax.fori_loop(0, rows, body, 0, unroll=ATTN_ROW_UNROLL)


def _attention(q, k, v, bias, *, seq_len, n_seq, first_block):
    a = q.shape[1]
    groups = a // LANES
    rows = seq_len // GRID_W
    assert rows >= WIN_ROWS and seq_len % GRID_W == 0 and rows % ATTN_ROW_UNROLL == 0
    kern = functools.partial(_attention_kernel, rows=rows)
    tok = lambda b, g: (first_block + b, g)
    return pl.pallas_call(
        kern,
        grid=(n_seq, groups),
        in_specs=[
            pl.BlockSpec((seq_len, LANES), tok),
            pl.BlockSpec((seq_len, LANES), tok),
            pl.BlockSpec((seq_len, LANES), tok),
            pl.BlockSpec((1,) + bias.shape[1:], lambda b, g: (g, 0, 0, 0)),
        ],
        out_specs=pl.BlockSpec((seq_len, LANES), lambda b, g: (b, g)),
        out_shape=jax.ShapeDtypeStruct((n_seq * seq_len, a), BF16),
        compiler_params=_params(("arbitrary", "arbitrary")),
        name="nbr_attention",
    )(q, k, v, bias)


def _mix_kernel(attn_s_ref, attn_p_ref, u_ref, utop_ref, ubot_ref, xs_ref, xp_ref, lng_ref, lnb_ref, wpool_ref,
                pscale_ref, wout_ref, g1_ref, b1_ref, wrc_ref, br_ref,
                h_ref, hs_ref, logit_ref, ubuf, *, n_sample_blocks, sample_len, n_sample_tokens, prompt_len,
                alpha):
    i = pl.program_id(0)
    tb = u_ref.shape[0]
    d = h_ref.shape[1]
    a = attn_s_ref.shape[1]
    pg = wpool_ref.shape[1]
    is_sample = i < n_sample_blocks
    tok0 = i * tb
    seq_len = jnp.where(is_sample, sample_len, prompt_len)
    local0 = jnp.where(is_sample, lax.rem(tok0, sample_len), tok0 - n_sample_tokens)
    top_ok = local0 > 0
    bot_ok = local0 + tb < seq_len

    ubuf[0:POOL_HALO, :] = jnp.where(top_ok, utop_ref[...].astype(F32), 0.0)
    ubuf[POOL_HALO:POOL_HALO + tb, :] = u_ref[...].astype(F32)
    ubuf[POOL_HALO + tb:, :] = jnp.where(bot_ok, ubot_ref[...].astype(F32), 0.0)
    pos = local0 + lax.broadcasted_iota(jnp.int32, (tb, 1), 0)

    attn = jnp.where(is_sample, attn_s_ref[...], attn_p_ref[...])
    mix = jnp.dot(attn, wout_ref[0:a, :], preferred_element_type=F32)
    for g, w in enumerate(POOL_WINDOWS):
        half = w // 2
        c0 = g * pg
        win = ubuf[POOL_HALO - half:POOL_HALO - half + tb, c0:c0 + pg]
        for dlt in range(-half + 1, half):
            win = win + ubuf[POOL_HALO + dlt:POOL_HALO + dlt + tb, c0:c0 + pg]
        cnt = (jnp.minimum(pos + half, seq_len) - jnp.maximum(pos - half, 0)).astype(F32)
        pooled = win / cnt - ubuf[POOL_HALO:POOL_HALO + tb, c0:c0 + pg]
        mixed = jnp.dot(pooled.astype(BF16), wpool_ref[g], preferred_element_type=F32)
        mixed = mixed * pscale_ref[:, c0:c0 + pg]
        mix = mix + jnp.dot(mixed.astype(BF16), wout_ref[a + c0:a + c0 + pg, :], preferred_element_type=F32)

    x = jnp.where(is_sample, xs_ref[...], xp_ref[...])
    h0 = _layer_norm(x, lng_ref[...], lnb_ref[...])
    h = _layer_norm(alpha * h0 + mix, g1_ref[...], b1_ref[...])
    h_ref[...] = h
    slabs = d // LANES
    for kk in range(slabs):
        hs_ref[pl.ds(kk, tb, stride=slabs), :] = h[:, kk * LANES:(kk + 1) * LANES]
    hi = h.astype(BF16)
    lo = (h - hi.astype(F32)).astype(BF16)
    hi_prod = jnp.dot(hi, wrc_ref[...], preferred_element_type=F32)
    logits = hi_prod[:, :LANES] + (jnp.dot(lo, wrc_ref[:, :LANES], preferred_element_type=F32)
                                   + hi_prod[:, LANES:])
    logit_ref[...] = logits + br_ref[...]


def _mix(attn_s, attn_p, u, xs, xp, ln_g, ln_b, w_pool_bf16, pool_scale, w_out_bf16, g1, b1, wr_cat, br,
         *, sample_len, prompt_len, alpha, block):
    ns, d = xs.shape
    np_ = xp.shape[0]
    n = ns + np_
    a = attn_s.shape[1]
    p = u.shape[1]
    nsb = ns // block
    slabs = d // LANES
    halo_blocks = block // POOL_HALO
    n_halo = n // POOL_HALO
    kern = functools.partial(_mix_kernel, n_sample_blocks=nsb, sample_len=sample_len, n_sample_tokens=ns,
                             prompt_len=prompt_len, alpha=alpha)
    row = lambda i: (i, 0)
    const = lambda i: (0, 0)
    srow = lambda i: (jnp.minimum(i, nsb - 1), 0)
    prow = lambda i: (jnp.maximum(i - nsb, 0), 0)
    return pl.pallas_call(
        kern,
        grid=(n // block,),
        in_specs=[
            pl.BlockSpec((block, a), srow),
            pl.BlockSpec((block, a), prow),
            pl.BlockSpec((block, p), row),
            pl.BlockSpec((POOL_HALO, p), lambda i: (jnp.maximum(i * halo_blocks - 1, 0), 0)),
            pl.BlockSpec((POOL_HALO, p), lambda i: (jnp.minimum((i + 1) * halo_blocks, n_halo - 1), 0)),
            pl.BlockSpec((block, d), srow),
            pl.BlockSpec((block, d), prow),
            pl.BlockSpec((1, d), const),
            pl.BlockSpec((1, d), const),
            pl.BlockSpec(w_pool_bf16.shape, lambda i: (0, 0, 0)),
            pl.BlockSpec((1, p), const),
            pl.BlockSpec(w_out_bf16.shape, const),
            pl.BlockSpec((1, d), const),
            pl.BlockSpec((1, d), const),
            pl.BlockSpec(wr_cat.shape, const),
            pl.BlockSpec((1, LANES), const),
        ],
        out_specs=[
            pl.BlockSpec((block, d), row),
            pl.BlockSpec((block * slabs, LANES), row),
            pl.BlockSpec((block, LANES), row),
        ],
        out_shape=[
            jax.ShapeDtypeStruct((n, d), F32),
            jax.ShapeDtypeStruct((n * slabs, LANES), F32),
            jax.ShapeDtypeStruct((n, LANES), F32),
        ],
        scratch_shapes=[pltpu.VMEM((block + 2 * POOL_HALO, p), F32)],
        compiler_params=_params(("arbitrary",)),
        name="mix_norm_logits",
    )(attn_s, attn_p, u, u, u, xs, xp, ln_g, ln_b, w_pool_bf16, pool_scale, w_out_bf16, g1, b1, wr_cat, br)


def _route_kernel(logit_ref, meta_ref, gate_ref, count_ref, carry, *, n_experts):
    i = pl.program_id(0)
    tb = logit_ref.shape[0]

    @pl.when(i == 0)
    def _():
        carry[...] = jnp.zeros_like(carry)

    lane = lax.broadcasted_iota(jnp.int32, (tb, LANES), 1)
    lane_f = lane.astype(F32)
    cur = jnp.where(lane < n_experts, logit_ref[...], -jnp.inf)
    vals, idxs, hots = [], [], []
    for _ in range(TOP_K):
        m = jnp.max(cur, axis=-1, keepdims=True)
        idx = jnp.min(jnp.where(cur == m, lane_f, float(LANES)), axis=-1, keepdims=True)
        hot = lane_f == idx
        cur = jnp.where(hot, -jnp.inf, cur)
        vals.append(m)
        idxs.append(idx)
        hots.append(hot)
    exps = [jnp.exp(v - vals[0]) for v in vals]
    den = exps[0]
    for e in exps[1:]:
        den = den + e
    chosen = jnp.zeros((tb, LANES), F32)
    for hot in hots:
        chosen = jnp.where(hot, 1.0, chosen)
    r_io = lax.broadcasted_iota(jnp.int32, (tb, tb), 0)
    c_io = lax.broadcasted_iota(jnp.int32, (tb, tb), 1)
    tri = jnp.where(r_io > c_io, 1.0, 0.0).astype(BF16)
    before = jnp.dot(tri, chosen.astype(BF16), preferred_element_type=F32) + carry[...]
    meta = jnp.zeros((tb, LANES), F32)
    gates = jnp.zeros((tb, LANES), F32)
    for k in range(TOP_K):
        rank = jnp.sum(jnp.where(hots[k], before, 0.0), axis=-1, keepdims=True)
        meta = jnp.where(lane == k, idxs[k], meta)
        meta = jnp.where(lane == TOP_K + k, rank, meta)
        gates = jnp.where(lane == k, exps[k] / den, gates)
    meta_ref[...] = meta.astype(jnp.int32)
    gate_ref[...] = gates
    total = carry[...] + jnp.sum(chosen, axis=0, keepdims=True)
    carry[...] = total
    count_ref[...] = total.astype(jnp.int32)


def _route(logits, *, n_experts, block):
    n = logits.shape[0]
    assert n * TOP_K < 2 ** 24
    row = lambda i: (i, 0)
    return pl.pallas_call(
        functools.partial(_route_kernel, n_experts=n_experts),
        grid=(n // block,),
        in_specs=[pl.BlockSpec((block, LANES), row)],
        out_specs=[
            pl.BlockSpec((block, LANES), row),
            pl.BlockSpec((block, LANES), row),
            pl.BlockSpec((1, LANES), lambda i: (0, 0)),
        ],
        out_shape=[
            jax.ShapeDtypeStruct((n, LANES), jnp.int32),
            jax.ShapeDtypeStruct((n, LANES), F32),
            jax.ShapeDtypeStruct((1, LANES), jnp.int32),
        ],
        scratch_shapes=[pltpu.VMEM((1, LANES), F32)],
        compiler_params=_params(("arbitrary",)),
        name="route_topk",
    )(logits)


def _row_copy(src_hbm, dst_vmem, sem, src_row, dst_row, slabs):
    return pltpu.make_async_copy(
        src_hbm.at[pl.ds(pl.multiple_of(src_row * slabs, slabs), slabs), :],
        dst_vmem.at[pl.ds(pl.multiple_of(dst_row * slabs, slabs), slabs), :],
        sem)


def _gather_start(src_hbm, dst_bufs, sem, idx_smem, n_rows, slabs):
    nb = len(dst_bufs)

    def issue(b, c):
        for s in range(ROW_DMA_UNROLL):
            r = b * ROW_DMA_UNROLL + s
            for k, dst in enumerate(dst_bufs):
                _row_copy(src_hbm, dst, sem, idx_smem[r * nb + k], r, slabs).start()
        return c

    lax.fori_loop(0, n_rows // ROW_DMA_UNROLL, issue, 0)


def _gather_wait(src_hbm, dst_bufs, sem, n_rows, slabs):
    def drain(b, c):
        for s in range(ROW_DMA_UNROLL):
            for dst in dst_bufs:
                _row_copy(src_hbm, dst, sem, 0, b * ROW_DMA_UNROLL + s, slabs).wait()
        return c

    lax.fori_loop(0, n_rows // ROW_DMA_UNROLL, drain, 0)


def _load_indices(idx_vmem_ref, idx_smem, sem):
    cp = pltpu.make_async_copy(idx_vmem_ref.at[0, 0], idx_smem, sem)
    cp.start()
    cp.wait()


def _experts_kernel(te_ref, nv_ref, first_ref, nxt_ref, tok_ref, tokn_ref, hs_hbm, wgu_hbm, wd_hbm, bgu_ref, bd_ref,
                    ys_ref, idx_smem, xland, xb16, acc, wg16, wu16, wd16, stg_g, stg_u, stg_d, sem_idx, sem_rows,
                    sem_w):
    i = pl.program_id(0)
    n_tiles = pl.num_programs(0)
    nv = nv_ref[i]
    next_used = (i + 1 < n_tiles) & (nv_ref[jnp.minimum(i + 1, n_tiles - 1)] > 0)
    tm, d = xb16.shape
    slabs = d // LANES
    n_chunks, _, fc = wg16.shape
    d_ff = n_chunks * fc

    def weight_copies(e, c, slot):
        col = pl.multiple_of(c * fc, fc)
        return (
            pltpu.make_async_copy(wgu_hbm.at[e, :, pl.ds(col, fc)], stg_g.at[slot], sem_w.at[slot]),
            pltpu.make_async_copy(wgu_hbm.at[e, :, pl.ds(d_ff + col, fc)], stg_u.at[slot], sem_w.at[slot]),
            pltpu.make_async_copy(wd_hbm.at[e, pl.ds(col, fc), :], stg_d.at[slot], sem_w.at[slot]),
        )

    def start_chunk(e, c, slot):
        for cp in weight_copies(e, c, slot):
            cp.start()

    @pl.when(nv > 0)
    def _():
        e = te_ref[i]
        first = first_ref[i] > 0

        @pl.when(first & (i == 0))
        def _():
            for c in range(WEIGHT_STAGES):
                start_chunk(e, c, c)

        @pl.when(i == 0)
        def _():
            _load_indices(tok_ref, idx_smem, sem_idx)
            _gather_start(hs_hbm, [xland], sem_rows, idx_smem, tm, slabs)

        _gather_wait(hs_hbm, [xland], sem_rows, tm, slabs)
        for kk in range(slabs):
            xb16[:, kk * LANES:(kk + 1) * LANES] = xland[pl.ds(kk, tm, stride=slabs), :].astype(BF16)

        @pl.when(next_used)
        def _():
            _load_indices(tokn_ref, idx_smem, sem_idx)
            _gather_start(hs_hbm, [xland], sem_rows, idx_smem, tm, slabs)

        acc[...] = jnp.broadcast_to(bd_ref[0], (tm, d))

        def chunk(c, carry):
            slot = lax.rem(c, WEIGHT_STAGES)

            @pl.when(first)
            def _():
                for cp_w in weight_copies(e, c, slot):
                    cp_w.wait()
                wg16[c] = stg_g[slot].astype(BF16)
                wu16[c] = stg_u[slot].astype(BF16)
                wd16[c] = stg_d[slot].astype(BF16)

                @pl.when(c + WEIGHT_STAGES < n_chunks)
                def _():
                    start_chunk(e, c + WEIGHT_STAGES, slot)

            x = xb16[...]
            gate = jnp.dot(x, wg16[c], preferred_element_type=F32) + bgu_ref[0, pl.ds(c, 1), :]
            up = jnp.dot(x, wu16[c], preferred_element_type=F32) + bgu_ref[0, pl.ds(n_chunks + c, 1), :]
            gate = jnp.minimum(gate, SWIGLU_LIMIT)
            up = jnp.clip(up, -SWIGLU_LIMIT, SWIGLU_LIMIT)
            act = (up + 1.0) * (gate * jax.nn.sigmoid(SWIGLU_ALPHA * gate))
            acc[...] += jnp.dot(act.astype(BF16), wd16[c], preferred_element_type=F32)
            return carry

        lax.fori_loop(0, n_chunks, chunk, 0)

        nxt = nxt_ref[i]

        @pl.when(nxt >= 0)
        def _():
            for c in range(WEIGHT_STAGES):
                start_chunk(nxt, c, c)

        for kk in range(slabs):
            ys_ref[pl.ds(kk, tm, stride=slabs), :] = acc[:, kk * LANES:(kk + 1) * LANES]

    @pl.when(nv == 0)
    def _():
        ys_ref[...] = jnp.zeros_like(ys_ref)


def _experts(tile_expert, tile_valid, tile_first, tile_next, slot_tok, h_slab, w_gate_up, b_gate_up, w_down, b_down,
             *, tile_rows, ff_chunk):
    n_tiles = tile_expert.shape[0]
    n_exp, d, two_f = w_gate_up.shape
    f = two_f // 2
    slabs = d // LANES
    n_chunks = f // ff_chunk
    assert n_chunks >= WEIGHT_STAGES and tile_rows % ROW_DMA_UNROLL == 0
    grid_spec = pltpu.PrefetchScalarGridSpec(
        num_scalar_prefetch=4,
        grid=(n_tiles,),
        in_specs=[
            pl.BlockSpec((1, 1, tile_rows), lambda i, te, nv, fs, nx: (i, 0, 0)),
            pl.BlockSpec((1, 1, tile_rows), lambda i, te, nv, fs, nx: (jnp.minimum(i + 1, n_tiles - 1), 0, 0)),
            pl.BlockSpec(memory_space=pl.ANY),
            pl.BlockSpec(memory_space=pl.ANY),
            pl.BlockSpec(memory_space=pl.ANY),
            pl.BlockSpec((1, 2 * n_chunks, ff_chunk), lambda i, te, nv, fs, nx: (te[i], 0, 0)),
            pl.BlockSpec((1, 1, d), lambda i, te, nv, fs, nx: (te[i], 0, 0)),
        ],
        out_specs=pl.BlockSpec((tile_rows * slabs, LANES), lambda i, te, nv, fs, nx: (i, 0)),
        scratch_shapes=[
            pltpu.SMEM((tile_rows,), jnp.int32),
            pltpu.VMEM((tile_rows * slabs, LANES), F32),
            pltpu.VMEM((tile_rows, d), BF16),
            pltpu.VMEM((tile_rows, d), F32),
            pltpu.VMEM((n_chunks, d, ff_chunk), BF16),
            pltpu.VMEM((n_chunks, d, ff_chunk), BF16),
            pltpu.VMEM((n_chunks, ff_chunk, d), BF16),
            pltpu.VMEM((WEIGHT_STAGES, d, ff_chunk), F32),
            pltpu.VMEM((WEIGHT_STAGES, d, ff_chunk), F32),
            pltpu.VMEM((WEIGHT_STAGES, ff_chunk, d), F32),
            pltpu.SemaphoreType.DMA,
            pltpu.SemaphoreType.DMA,
            pltpu.SemaphoreType.DMA((WEIGHT_STAGES,)),
        ],
    )
    return pl.pallas_call(
        _experts_kernel,
        grid_spec=grid_spec,
        out_shape=jax.ShapeDtypeStruct((n_tiles * tile_rows * slabs, LANES), F32),
        compiler_params=_params(("arbitrary",)),
        name="moe_experts",
    )(tile_expert, tile_valid, tile_first, tile_next, slot_tok, slot_tok, h_slab, w_gate_up, w_down, b_gate_up,
      b_down)


def _combine_kernel(dest_ref, destn_ref, gate_ref, ys_hbm, h_ref, g2_ref, b2_ref, os_ref, op_ref, idx_smem, ybuf,
                    fslab, fstd, sem_idx, sem_rows, *, n_sample_blocks, alpha):
    i = pl.program_id(0)
    n_blocks = pl.num_programs(0)
    tb, d = h_ref.shape
    slabs = d // LANES
    slot = lax.rem(i, 2)
    bufs = lambda s: [ybuf.at[s, k] for k in range(TOP_K)]

    @pl.when(i == 0)
    def _():
        _load_indices(dest_ref, idx_smem, sem_idx)
        _gather_start(ys_hbm, bufs(0), sem_rows.at[0], idx_smem, tb, slabs)

    @pl.when(i + 1 < n_blocks)
    def _():
        _load_indices(destn_ref, idx_smem, sem_idx)
        _gather_start(ys_hbm, bufs(1 - slot), sem_rows.at[1 - slot], idx_smem, tb, slabs)

    _gather_wait(ys_hbm, bufs(slot), sem_rows.at[slot], tb, slabs)

    gates = gate_ref[...]
    f = gates[:, 0:1] * ybuf[slot, 0]
    for k in range(1, TOP_K):
        f = f + gates[:, k:k + 1] * ybuf[slot, k]
    fslab[...] = f
    for kk in range(slabs):
        fstd[:, kk * LANES:(kk + 1) * LANES] = fslab[pl.ds(kk, tb, stride=slabs), :]
    y = _layer_norm(alpha * h_ref[...] + fstd[...], g2_ref[...], b2_ref[...])

    @pl.when(i < n_sample_blocks)
    def _():
        os_ref[...] = y

    @pl.when(i >= n_sample_blocks)
    def _():
        op_ref[...] = y


def _combine(dest, gates_slab, y_slab, h, g2, b2, *, n_sample, alpha, block):
    n, d = h.shape
    slabs = d // LANES
    nsb = n_sample // block
    assert block % ROW_DMA_UNROLL == 0
    kern = functools.partial(_combine_kernel, n_sample_blocks=nsb, alpha=alpha)
    const = lambda i: (0, 0)
    return pl.pallas_call(
        kern,
        grid=(n // block,),
        in_specs=[
            pl.BlockSpec((1, 1, block * TOP_K), lambda i: (i, 0, 0)),
            pl.BlockSpec((1, 1, block * TOP_K), lambda i: (jnp.minimum(i + 1, n // block - 1), 0, 0)),
            pl.BlockSpec((block * slabs, TOP_K), lambda i: (i, 0)),
            pl.BlockSpec(memory_space=pl.ANY),
            pl.BlockSpec((block, d), lambda i: (i, 0)),
            pl.BlockSpec((1, d), const),
            pl.BlockSpec((1, d), const),
        ],
        out_specs=[
            pl.BlockSpec((block, d), lambda i: (jnp.minimum(i, nsb - 1), 0)),
            pl.BlockSpec((block, d), lambda i: (jnp.maximum(i - nsb, 0), 0)),
        ],
        out_shape=[
            jax.ShapeDtypeStruct((n_sample, d), F32),
            jax.ShapeDtypeStruct((n - n_sample, d), F32),
        ],
        scratch_shapes=[
            pltpu.SMEM((block * TOP_K,), jnp.int32),
            pltpu.VMEM((2, TOP_K, block * slabs, LANES), F32),
            pltpu.VMEM((block * slabs, LANES), F32),
            pltpu.VMEM((block, d), F32),
            pltpu.SemaphoreType.DMA,
            pltpu.SemaphoreType.DMA((2,)),
        ],
        compiler_params=_params(("arbitrary",)),
        name="combine_norm",
    )(dest, dest, gates_slab, y_slab, h, g2, b2)


def _tiles(n_tokens, d_ff):
    return dict(
        token_block=min(256, n_tokens),
        route_block=min(512, n_tokens),
        expert_rows=min(512, n_tokens),
        ff_chunk=min(256, d_ff),
    )


def kernel(x_prompt, x_sample, ln_in_g, ln_in_b, w_in, rpb, w_pool, pool_scale, w_out, ln1_g, ln1_b, w_router,
           b_router, w_gate_up, b_gate_up, w_down, b_down, ln2_g, ln2_b):
    depth = w_in.shape[0]
    assert depth == 1, "single-layer trunk"
    d = x_prompt.shape[-1]
    prompt_len = x_prompt.shape[1]
    sample_len = x_sample.shape[1]
    xs = x_sample.reshape(-1, d)
    xp = x_prompt.reshape(-1, d)
    ns, np_ = xs.shape[0], xp.shape[0]
    n = ns + np_
    assert x_prompt.shape[0] == 1 and ns % prompt_len == 0
    n_heads = rpb.shape[1]
    attn_width = n_heads * HEAD_DIM
    n_experts = w_router.shape[-1]
    d_ff = w_down.shape[2]
    alpha = float((2 * depth) ** 0.25)
    t = _tiles(n, d_ff)
    tb = t["token_block"]
    slabs = d // LANES
    row = lambda v: v.reshape(1, -1).astype(F32)

    q, k, v, u = _in_proj(xs, xp, row(ln_in_g), row(ln_in_b), w_in[0].astype(BF16), attn_width=attn_width, block=tb)

    bias = _attn_bias_table(rpb[0])
    attn_s = _attention(q, k, v, bias, seq_len=sample_len, n_seq=ns // sample_len, first_block=0)
    attn_p = _attention(q, k, v, bias, seq_len=prompt_len, n_seq=1, first_block=ns // prompt_len)

    wr = jnp.pad(w_router[0].astype(F32), ((0, 0), (0, LANES - n_experts)))
    wr_hi = wr.astype(BF16)
    wr_lo = (wr - wr_hi.astype(F32)).astype(BF16)
    wr_cat = jnp.concatenate([wr_hi, wr_lo], axis=1)
    br = jnp.pad(b_router[0].astype(F32), (0, LANES - n_experts)).reshape(1, LANES)
    h, h_slab, logits = _mix(attn_s, attn_p, u, xs, xp, row(ln_in_g), row(ln_in_b), w_pool[0].astype(BF16),
                             row(pool_scale[0]), w_out[0].astype(BF16), row(ln1_g[0]), row(ln1_b[0]), wr_cat,
                             br, sample_len=sample_len, prompt_len=prompt_len, alpha=alpha, block=tb)

    meta, gates_pad, counts_pad = _route(logits, n_experts=n_experts, block=t["route_block"])
    top_idx = meta[:, :TOP_K]
    rank = meta[:, TOP_K:2 * TOP_K]
    gates = gates_pad[:, :TOP_K]
    counts = counts_pad[0, :n_experts]
    tm = t["expert_rows"]
    padded = (counts + tm - 1) // tm * tm
    pend = jnp.cumsum(padded)
    poff = pend - padded
    expert_ids = jnp.arange(n_experts, dtype=jnp.int32)
    base = jnp.sum(jnp.where(top_idx[:, :, None] == expert_ids, poff, 0), axis=-1)
    dest = (base + rank).astype(jnp.int32)
    n_tiles = (n * TOP_K) // tm + n_experts
    tok_of_pair = jnp.repeat(jnp.arange(n, dtype=jnp.int32), TOP_K)
    slot_tok = jnp.zeros((n_tiles * tm,), jnp.int32).at[dest.reshape(-1)].set(tok_of_pair)
    tile_start = jnp.arange(n_tiles, dtype=jnp.int32) * tm
    tile_expert = jnp.minimum(jnp.sum(tile_start[:, None] >= pend[None, :], axis=1), n_experts - 1).astype(jnp.int32)
    tile_valid = jnp.clip((poff + counts)[tile_expert] - tile_start, 0, tm).astype(jnp.int32)
    used = tile_valid > 0
    prev_expert = jnp.concatenate([jnp.full((1,), -1, jnp.int32), tile_expert[:-1]])
    tile_first = (used & (tile_expert != prev_expert)).astype(jnp.int32)
    next_first = jnp.concatenate([tile_first[1:], jnp.zeros((1,), jnp.int32)])
    next_expert = jnp.concatenate([tile_expert[1:], jnp.zeros((1,), jnp.int32)])
    tile_next = jnp.where(next_first > 0, next_expert, -1).astype(jnp.int32)

    y_slab = _experts(tile_expert, tile_valid, tile_first, tile_next, slot_tok.reshape(n_tiles, 1, tm), h_slab,
                      w_gate_up[0], b_gate_up[0].reshape(n_experts, -1, t["ff_chunk"]), w_down[0],
                      b_down[0].reshape(n_experts, 1, -1), tile_rows=tm, ff_chunk=t["ff_chunk"])

    gates_slab = jnp.repeat(gates, slabs, axis=0)
    out_s, out_p = _combine(dest.reshape(n // tb, 1, tb * TOP_K), gates_slab, y_slab, h,
                            row(ln2_g[0]), row(ln2_b[0]), n_sample=ns, alpha=alpha, block=tb)
    return out_p.reshape(x_prompt.shape), out_s.reshape(x_sample.shape)
```

```python
import functools

import jax
import jax.numpy as jnp
from jax import lax
from jax.experimental import pallas as pl
from jax.experimental.pallas import tpu as pltpu

F32 = jnp.float32
BF16 = jnp.bfloat16

LANES = 128
VMEM_LIMIT_BYTES = 60 * 1024 * 1024

GRID_W = 64
HEAD_DIM = 32
WIN_ROWS = 8
WIN_COLS = 16
POOL_WINDOWS = (2, 4, 8, 16)
TOP_K = 4
SWIGLU_LIMIT = 7.0
SWIGLU_ALPHA = 1.702
LN_EPS = 1e-5
NEG_BIG = -1e30

HEADS_PER_GROUP = LANES // HEAD_DIM
POOL_HALO = 16
ROW_DMA_UNROLL = 8
ATTN_ROW_UNROLL = 4
WEIGHT_STAGES = 2


def _layer_norm(x, g, b):
    mu = jnp.mean(x, axis=-1, keepdims=True)
    xc = x - mu
    var = jnp.mean(xc * xc, axis=-1, keepdims=True)
    return xc * lax.rsqrt(var + LN_EPS) * g + b


def _params(semantics):
    return pltpu.CompilerParams(dimension_semantics=semantics, vmem_limit_bytes=VMEM_LIMIT_BYTES)


def _in_proj_kernel(xs_ref, xp_ref, g_ref, b_ref, w_ref, q_ref, k_ref, v_ref, u_ref, *, n_sample_blocks,
                    attn_width, q_scale):
    i = pl.program_id(0)
    x = jnp.where(i < n_sample_blocks, xs_ref[...], xp_ref[...])
    h = _layer_norm(x, g_ref[...], b_ref[...])
    proj = jnp.dot(h.astype(BF16), w_ref[...], preferred_element_type=F32)
    a = attn_width
    q_ref[...] = (proj[:, :a] * q_scale).astype(BF16)
    k_ref[...] = proj[:, a:2 * a].astype(BF16)
    v_ref[...] = proj[:, 2 * a:3 * a].astype(BF16)
    u_ref[...] = proj[:, 3 * a:].astype(BF16)


def _in_proj(xs, xp, ln_g, ln_b, w_in_bf16, *, attn_width, block):
    ns, d = xs.shape
    np_ = xp.shape[0]
    n = ns + np_
    in_width = w_in_bf16.shape[1]
    pool_width = in_width - 3 * attn_width
    nsb = ns // block
    kern = functools.partial(_in_proj_kernel, n_sample_blocks=nsb, attn_width=attn_width,
                             q_scale=HEAD_DIM ** -0.5)
    row = lambda i: (i, 0)
    const = lambda i: (0, 0)
    return pl.pallas_call(
        kern,
        grid=(n // block,),
        in_specs=[
            pl.BlockSpec((block, d), lambda i: (jnp.minimum(i, nsb - 1), 0)),
            pl.BlockSpec((block, d), lambda i: (jnp.maximum(i - nsb, 0), 0)),
            pl.BlockSpec((1, d), const),
            pl.BlockSpec((1, d), const),
            pl.BlockSpec((d, in_width), const),
        ],
        out_specs=[
            pl.BlockSpec((block, attn_width), row),
            pl.BlockSpec((block, attn_width), row),
            pl.BlockSpec((block, attn_width), row),
            pl.BlockSpec((block, pool_width), row),
        ],
        out_shape=[
            jax.ShapeDtypeStruct((n, attn_width), BF16),
            jax.ShapeDtypeStruct((n, attn_width), BF16),
            jax.ShapeDtypeStruct((n, attn_width), BF16),
            jax.ShapeDtypeStruct((n, pool_width), BF16),
        ],
        compiler_params=_params(("arbitrary",)),
        name="in_proj",
    )(xs, xp, ln_g, ln_b, w_in_bf16)


def _attn_bias_table(rpb):
    n_heads = rpb.shape[0]
    cols = jnp.arange(GRID_W)
    col_start = jnp.clip(cols - WIN_COLS // 2, 0, GRID_W - WIN_COLS)
    in_win = (cols[None, :] >= col_start[:, None]) & (cols[None, :] < col_start[:, None] + WIN_COLS)
    dc = cols[None, :] - cols[:, None] + (WIN_COLS - 1)
    select = (dc[None] == jnp.arange(2 * WIN_COLS - 1)[:, None, None]).astype(F32)
    toep = jnp.einsum('hrd,dqk->hrqk', rpb.astype(F32), select, precision=lax.Precision.HIGHEST)
    toep = jnp.where(in_win[None, None], toep, NEG_BIG)
    t = jnp.stack([toep[:, WIN_ROWS - 1 - c:2 * WIN_ROWS - 1 - c] for c in range(WIN_ROWS)], axis=1)
    t = jnp.transpose(t, (0, 1, 3, 2, 4))
    g = n_heads // HEADS_PER_GROUP
    t = t.reshape(g, HEADS_PER_GROUP, WIN_ROWS, GRID_W, WIN_ROWS * GRID_W)
    t = jnp.transpose(t, (0, 2, 1, 3, 4))
    return t.reshape(g, WIN_ROWS, HEADS_PER_GROUP * GRID_W, WIN_ROWS * GRID_W)


def _attention_kernel(q_ref, k_ref, v_ref, bias_ref, o_ref, *, rows):
    lane_head = lax.broadcasted_iota(jnp.int32, (1, LANES), 1) // HEAD_DIM
    kv_rows = WIN_ROWS * GRID_W

    def body(r, carry):
        rs = jnp.clip(r - WIN_ROWS // 2, 0, rows - WIN_ROWS)
        case = jnp.minimum(r, WIN_ROWS // 2) + jnp.maximum(r - (rows - WIN_ROWS // 2), 0)
        q = q_ref[pl.ds(pl.multiple_of(r * GRID_W, GRID_W), GRID_W), :]
        kb = k_ref[pl.ds(pl.multiple_of(rs * GRID_W, GRID_W), kv_rows), :]
        vb = v_ref[pl.ds(pl.multiple_of(rs * GRID_W, GRID_W), kv_rows), :]
        zero = jnp.zeros_like(q)
        qs = jnp.concatenate([jnp.where(lane_head == h, q, zero) for h in range(HEADS_PER_GROUP)], axis=0)
        s = lax.dot_general(qs, kb, (((1,), (1,)), ((), ())), preferred_element_type=F32)
        s = s + bias_ref[0, case]
        m = jnp.max(s, axis=-1, keepdims=True)
        p = jnp.exp(s - m)
        l = jnp.sum(p, axis=-1, keepdims=True)
        o = jnp.dot(p.astype(BF16), vb, preferred_element_type=F32) / l
        out = jnp.zeros((GRID_W, LANES), F32)
        for h in range(HEADS_PER_GROUP):
            out = jnp.where(lane_head == h, o[h * GRID_W:(h + 1) * GRID_W], out)
        o_ref[pl.ds(pl.multiple_of(r * GRID_W, GRID_W), GRID_W), :] = out.astype(o_ref.dtype)
        return carry

    lax.fori_loop(0, rows, body, 0, unroll=ATTN_ROW_UNROLL)


def _attention(q, k, v, bias, *, seq_len, n_seq, first_block):
    a = q.shape[1]
    groups = a // LANES
    rows = seq_len // GRID_W
    assert rows >= WIN_ROWS and seq_len % GRID_W == 0 and rows % ATTN_ROW_UNROLL == 0
    kern = functools.partial(_attention_kernel, rows=rows)
    tok = lambda b, g: (first_block + b, g)
    return pl.pallas_call(
        kern,
        grid=(n_seq, groups),
        in_specs=[
            pl.BlockSpec((seq_len, LANES), tok),
            pl.BlockSpec((seq_len, LANES), tok),
            pl.BlockSpec((seq_len, LANES), tok),
            pl.BlockSpec((1,) + bias.shape[1:], lambda b, g: (g, 0, 0, 0)),
        ],
        out_specs=pl.BlockSpec((seq_len, LANES), lambda b, g: (b, g)),
        out_shape=jax.ShapeDtypeStruct((n_seq * seq_len, a), BF16),
        compiler_params=_params(("arbitrary", "arbitrary")),
        name="nbr_attention",
    )(q, k, v, bias)


def _mix_kernel(attn_s_ref, attn_p_ref, u_ref, utop_ref, ubot_ref, xs_ref, xp_ref, lng_ref, lnb_ref, wpool_ref,
                pscale_ref, wout_ref, g1_ref, b1_ref, wrc_ref, br_ref,
                h_ref, hs_ref, logit_ref, ubuf, *, n_sample_blocks, sample_len, n_sample_tokens, prompt_len,
                alpha):
    i = pl.program_id(0)
    tb = u_ref.shape[0]
    d = h_ref.shape[1]
    a = attn_s_ref.shape[1]
    pg = wpool_ref.shape[1]
    is_sample = i < n_sample_blocks
    tok0 = i * tb
    seq_len = jnp.where(is_sample, sample_len, prompt_len)
    local0 = jnp.where(is_sample, lax.rem(tok0, sample_len), tok0 - n_sample_tokens)
    top_ok = local0 > 0
    bot_ok = local0 + tb < seq_len

    ubuf[0:POOL_HALO, :] = jnp.where(top_ok, utop_ref[...].astype(F32), 0.0)
    ubuf[POOL_HALO:POOL_HALO + tb, :] = u_ref[...].astype(F32)
    ubuf[POOL_HALO + tb:, :] = jnp.where(bot_ok, ubot_ref[...].astype(F32), 0.0)
    pos = local0 + lax.broadcasted_iota(jnp.int32, (tb, 1), 0)

    attn = jnp.where(is_sample, attn_s_ref[...], attn_p_ref[...])
    mix = jnp.dot(attn, wout_ref[0:a, :], preferred_element_type=F32)
    for g, w in enumerate(POOL_WINDOWS):
        half = w // 2
        c0 = g * pg
        win = ubuf[POOL_HALO - half:POOL_HALO - half + tb, c0:c0 + pg]
        for dlt in range(-half + 1, half):
            win = win + ubuf[POOL_HALO + dlt:POOL_HALO + dlt + tb, c0:c0 + pg]
        cnt = (jnp.minimum(pos + half, seq_len) - jnp.maximum(pos - half, 0)).astype(F32)
        pooled = win / cnt - ubuf[POOL_HALO:POOL_HALO + tb, c0:c0 + pg]
        mixed = jnp.dot(pooled.astype(BF16), wpool_ref[g], preferred_element_type=F32)
        mixed = mixed * pscale_ref[:, c0:c0 + pg]
        mix = mix + jnp.dot(mixed.astype(BF16), wout_ref[a + c0:a + c0 + pg, :], preferred_element_type=F32)

    x = jnp.where(is_sample, xs_ref[...], xp_ref[...])
    h0 = _layer_norm(x, lng_ref[...], lnb_ref[...])
    h = _layer_norm(alpha * h0 + mix, g1_ref[...], b1_ref[...])
    h_ref[...] = h
    slabs = d // LANES
    for kk in range(slabs):
        hs_ref[pl.ds(kk, tb, stride=slabs), :] = h[:, kk * LANES:(kk + 1) * LANES]
    hi = h.astype(BF16)
    lo = (h - hi.astype(F32)).astype(BF16)
    hi_prod = jnp.dot(hi, wrc_ref[...], preferred_element_type=F32)
    logits = hi_prod[:, :LANES] + (jnp.dot(lo, wrc_ref[:, :LANES], preferred_element_type=F32)
                                   + hi_prod[:, LANES:])
    logit_ref[...] = logits + br_ref[...]


def _mix(attn_s, attn_p, u, xs, xp, ln_g, ln_b, w_pool_bf16, pool_scale, w_out_bf16, g1, b1, wr_cat, br,
         *, sample_len, prompt_len, alpha, block):
    ns, d = xs.shape
    np_ = xp.shape[0]
    n = ns + np_
    a = attn_s.shape[1]
    p = u.shape[1]
    nsb = ns // block
    slabs = d // LANES
    halo_blocks = block // POOL_HALO
    n_halo = n // POOL_HALO
    kern = functools.partial(_mix_kernel, n_sample_blocks=nsb, sample_len=sample_len, n_sample_tokens=ns,
                             prompt_len=prompt_len, alpha=alpha)
    row = lambda i: (i, 0)
    const = lambda i: (0, 0)
    srow = lambda i: (jnp.minimum(i, nsb - 1), 0)
    prow = lambda i: (jnp.maximum(i - nsb, 0), 0)
    return pl.pallas_call(
        kern,
        grid=(n // block,),
        in_specs=[
            pl.BlockSpec((block, a), srow),
            pl.BlockSpec((block, a), prow),
            pl.BlockSpec((block, p), row),
            pl.BlockSpec((POOL_HALO, p), lambda i: (jnp.maximum(i * halo_blocks - 1, 0), 0)),
            pl.BlockSpec((POOL_HALO, p), lambda i: (jnp.minimum((i + 1) * halo_blocks, n_halo - 1), 0)),
            pl.BlockSpec((block, d), srow),
            pl.BlockSpec((block, d), prow),
            pl.BlockSpec((1, d), const),
            pl.BlockSpec((1, d), const),
            pl.BlockSpec(w_pool_bf16.shape, lambda i: (0, 0, 0)),
            pl.BlockSpec((1, p), const),
            pl.BlockSpec(w_out_bf16.shape, const),
            pl.BlockSpec((1, d), const),
            pl.BlockSpec((1, d), const),
            pl.BlockSpec(wr_cat.shape, const),
            pl.BlockSpec((1, LANES), const),
        ],
        out_specs=[
            pl.BlockSpec((block, d), row),
            pl.BlockSpec((block * slabs, LANES), row),
            pl.BlockSpec((block, LANES), row),
        ],
        out_shape=[
            jax.ShapeDtypeStruct((n, d), F32),
            jax.ShapeDtypeStruct((n * slabs, LANES), F32),
            jax.ShapeDtypeStruct((n, LANES), F32),
        ],
        scratch_shapes=[pltpu.VMEM((block + 2 * POOL_HALO, p), F32)],
        compiler_params=_params(("arbitrary",)),
        name="mix_norm_logits",
    )(attn_s, attn_p, u, u, u, xs, xp, ln_g, ln_b, w_pool_bf16, pool_scale, w_out_bf16, g1, b1, wr_cat, br)


def _route_kernel(logit_ref, meta_ref, gate_ref, count_ref, carry, *, n_experts):
    i = pl.program_id(0)
    tb = logit_ref.shape[0]

    @pl.when(i == 0)
    def _():
        carry[...] = jnp.zeros_like(carry)

    lane = lax.broadcasted_iota(jnp.int32, (tb, LANES), 1)
    lane_f = lane.astype(F32)
    cur = jnp.where(lane < n_experts, logit_ref[...], -jnp.inf)
    vals, idxs, hots = [], [], []
    for _ in range(TOP_K):
        m = jnp.max(cur, axis=-1, keepdims=True)
        idx = jnp.min(jnp.where(cur == m, lane_f, float(LANES)), axis=-1, keepdims=True)
        hot = lane_f == idx
        cur = jnp.where(hot, -jnp.inf, cur)
        vals.append(m)
        idxs.append(idx)
        hots.append(hot)
    exps = [jnp.exp(v - vals[0]) for v in vals]
    den = exps[0]
    for e in exps[1:]:
        den = den + e
    chosen = jnp.zeros((tb, LANES), F32)
    for hot in hots:
        chosen = jnp.where(hot, 1.0, chosen)
    r_io = lax.broadcasted_iota(jnp.int32, (tb, tb), 0)
    c_io = lax.broadcasted_iota(jnp.int32, (tb, tb), 1)
    tri = jnp.where(r_io > c_io, 1.0, 0.0).astype(BF16)
    before = jnp.dot(tri, chosen.astype(BF16), preferred_element_type=F32) + carry[...]
    meta = jnp.zeros((tb, LANES), F32)
    gates = jnp.zeros((tb, LANES), F32)
    for k in range(TOP_K):
        rank = jnp.sum(jnp.where(hots[k], before, 0.0), axis=-1, keepdims=True)
        meta = jnp.where(lane == k, idxs[k], meta)
        meta = jnp.where(lane == TOP_K + k, rank, meta)
        gates = jnp.where(lane == k, exps[k] / den, gates)
    meta_ref[...] = meta.astype(jnp.int32)
    gate_ref[...] = gates
    total = carry[...] + jnp.sum(chosen, axis=0, keepdims=True)
    carry[...] = total
    count_ref[...] = total.astype(jnp.int32)


def _route(logits, *, n_experts, block):
    n = logits.shape[0]
    assert n * TOP_K < 2 ** 24
    row = lambda i: (i, 0)
    return pl.pallas_call(
        functools.partial(_route_kernel, n_experts=n_experts),
        grid=(n // block,),
        in_specs=[pl.BlockSpec((block, LANES), row)],
        out_specs=[
            pl.BlockSpec((block, LANES), row),
            pl.BlockSpec((block, LANES), row),
            pl.BlockSpec((1, LANES), lambda i: (0, 0)),
        ],
        out_shape=[
            jax.ShapeDtypeStruct((n, LANES), jnp.int32),
            jax.ShapeDtypeStruct((n, LANES), F32),
            jax.ShapeDtypeStruct((1, LANES), jnp.int32),
        ],
        scratch_shapes=[pltpu.VMEM((1, LANES), F32)],
        compiler_params=_params(("arbitrary",)),
        name="route_topk",
    )(logits)


def _row_copy(src_hbm, dst_vmem, sem, src_row, dst_row, slabs):
    return pltpu.make_async_copy(
        src_hbm.at[pl.ds(pl.multiple_of(src_row * slabs, slabs), slabs), :],
        dst_vmem.at[pl.ds(pl.multiple_of(dst_row * slabs, slabs), slabs), :],
        sem)


def _gather_start(src_hbm, dst_bufs, sem, idx_smem, n_rows, slabs):
    nb = len(dst_bufs)

    def issue(b, c):
        for s in range(ROW_DMA_UNROLL):
            r = b * ROW_DMA_UNROLL + s
            for k, dst in enumerate(dst_bufs):
                _row_copy(src_hbm, dst, sem, idx_smem[r * nb + k], r, slabs).start()
        return c

    lax.fori_loop(0, n_rows // ROW_DMA_UNROLL, issue, 0)


def _gather_wait(src_hbm, dst_bufs, sem, n_rows, slabs):
    def drain(b, c):
        for s in range(ROW_DMA_UNROLL):
            for dst in dst_bufs:
                _row_copy(src_hbm, dst, sem, 0, b * ROW_DMA_UNROLL + s, slabs).wait()
        return c

    lax.fori_loop(0, n_rows // ROW_DMA_UNROLL, drain, 0)


def _load_indices(idx_vmem_ref, idx_smem, sem):
    cp = pltpu.make_async_copy(idx_vmem_ref.at[0, 0], idx_smem, sem)
    cp.start()
    cp.wait()


def _experts_kernel(te_ref, nv_ref, first_ref, nxt_ref, tok_ref, tokn_ref, hs_hbm, wgu_hbm, wd_hbm, bgu_ref, bd_ref,
                    ys_ref, idx_smem, xland, xb16, acc, wg16, wu16, wd16, stg_g, stg_u, stg_d, sem_idx, sem_rows,
                    sem_w):
    i = pl.program_id(0)
    n_tiles = pl.num_programs(0)
    nv = nv_ref[i]
    next_used = (i + 1 < n_tiles) & (nv_ref[jnp.minimum(i + 1, n_tiles - 1)] > 0)
    tm, d = acc.shape
    slabs = d // LANES
    n_chunks, _, fc = wg16.shape
    d_ff = n_chunks * fc
    par = lax.rem(i, 2)
    part = tm // n_chunks

    def convert_part(dst_slot, pc):
        row0 = pl.multiple_of(pc * part, part)
        for kk in range(slabs):
            xb16[dst_slot, pl.ds(row0, part), kk * LANES:(kk + 1) * LANES] = (
                xland[pl.ds(row0 * slabs + kk, part, stride=slabs), :].astype(BF16))

    def weight_copies(e, c, slot):
        col = c * fc if isinstance(c, int) else pl.multiple_of(c * fc, fc)
        return (
            pltpu.make_async_copy(wgu_hbm.at[e, :, pl.ds(col, fc)], stg_g.at[slot], sem_w.at[slot]),
            pltpu.make_async_copy(wgu_hbm.at[e, :, pl.ds(d_ff + col, fc)], stg_u.at[slot], sem_w.at[slot]),
            pltpu.make_async_copy(wd_hbm.at[e, pl.ds(col, fc), :], stg_d.at[slot], sem_w.at[slot]),
        )

    def start_chunk(e, c, slot):
        for cp in weight_copies(e, c, slot):
            cp.start()

    @pl.when(nv > 0)
    def _():
        e = te_ref[i]
        first = first_ref[i] > 0

        @pl.when(first & (i == 0))
        def _():
            for c in range(WEIGHT_STAGES):
                start_chunk(e, c, c)

        @pl.when(i == 0)
        def _():
            _load_indices(tok_ref, idx_smem, sem_idx)
            _gather_start(hs_hbm, [xland], sem_rows, idx_smem, tm, slabs)
            _gather_wait(hs_hbm, [xland], sem_rows, tm, slabs)
            for pc in range(n_chunks):
                convert_part(0, pc)

        @pl.when(next_used)
        def _():
            _load_indices(tokn_ref, idx_smem, sem_idx)
            _gather_start(hs_hbm, [xland], sem_rows, idx_smem, tm, slabs)

        acc[...] = jnp.broadcast_to(bd_ref[0], (tm, d))

        def chunk(c, carry, convert):
            slot = c % WEIGHT_STAGES if isinstance(c, int) else lax.rem(c, WEIGHT_STAGES)

            @pl.when(first)
            def _():
                for cp_w in weight_copies(e, c, slot):
                    cp_w.wait()
                wg16[c] = stg_g[slot].astype(BF16)
                wu16[c] = stg_u[slot].astype(BF16)
                wd16[c] = stg_d[slot].astype(BF16)

                @pl.when(c + WEIGHT_STAGES < n_chunks)
                def _():
                    start_chunk(e, c + WEIGHT_STAGES, slot)

            if convert:
                convert_part(1 - par, c - 1)
            x = xb16[par]
            gate = jnp.dot(x, wg16[c], preferred_element_type=F32) + bgu_ref[0, pl.ds(c, 1), :]
            up = jnp.dot(x, wu16[c], preferred_element_type=F32) + bgu_ref[0, pl.ds(n_chunks + c, 1), :]
            gate = jnp.minimum(gate, SWIGLU_LIMIT)
            up = jnp.clip(up, -SWIGLU_LIMIT, SWIGLU_LIMIT)
            act = (up + 1.0) * (gate * jax.nn.sigmoid(SWIGLU_ALPHA * gate))
            acc[...] += jnp.dot(act.astype(BF16), wd16[c], preferred_element_type=F32)
            return carry

        chunk(0, 0, False)

        @pl.when(next_used)
        def _():
            _gather_wait(hs_hbm, [xland], sem_rows, tm, slabs)

        lax.fori_loop(1, n_chunks, lambda c, carry: chunk(c, carry, True), 0)
        convert_part(1 - par, n_chunks - 1)

        nxt = nxt_ref[i]

        @pl.when(nxt >= 0)
        def _():
            for c in range(WEIGHT_STAGES):
                start_chunk(nxt, c, c)

        for kk in range(slabs):
            ys_ref[pl.ds(kk, tm, stride=slabs), :] = acc[:, kk * LANES:(kk + 1) * LANES]

    @pl.when(nv == 0)
    def _():
        ys_ref[...] = jnp.zeros_like(ys_ref)


def _experts(tile_expert, tile_valid, tile_first, tile_next, slot_tok, h_slab, w_gate_up, b_gate_up, w_down, b_down,
             *, tile_rows, ff_chunk):
    n_tiles = tile_expert.shape[0]
    n_exp, d, two_f = w_gate_up.shape
    f = two_f // 2
    slabs = d // LANES
    n_chunks = f // ff_chunk
    assert n_chunks >= WEIGHT_STAGES and tile_rows % ROW_DMA_UNROLL == 0
    assert tile_rows % (16 * n_chunks) == 0
    grid_spec = pltpu.PrefetchScalarGridSpec(
        num_scalar_prefetch=4,
        grid=(n_tiles,),
        in_specs=[
            pl.BlockSpec((1, 1, tile_rows), lambda i, te, nv, fs, nx: (i, 0, 0)),
            pl.BlockSpec((1, 1, tile_rows), lambda i, te, nv, fs, nx: (jnp.minimum(i + 1, n_tiles - 1), 0, 0)),
            pl.BlockSpec(memory_space=pl.ANY),
            pl.BlockSpec(memory_space=pl.ANY),
            pl.BlockSpec(memory_space=pl.ANY),
            pl.BlockSpec((1, 2 * n_chunks, ff_chunk), lambda i, te, nv, fs, nx: (te[i], 0, 0)),
            pl.BlockSpec((1, 1, d), lambda i, te, nv, fs, nx: (te[i], 0, 0)),
        ],
        out_specs=pl.BlockSpec((tile_rows * slabs, LANES), lambda i, te, nv, fs, nx: (i, 0)),
        scratch_shapes=[
            pltpu.SMEM((tile_rows,), jnp.int32),
            pltpu.VMEM((tile_rows * slabs, LANES), F32),
            pltpu.VMEM((2, tile_rows, d), BF16),
            pltpu.VMEM((tile_rows, d), F32),
            pltpu.VMEM((n_chunks, d, ff_chunk), BF16),
            pltpu.VMEM((n_chunks, d, ff_chunk), BF16),
            pltpu.VMEM((n_chunks, ff_chunk, d), BF16),
            pltpu.VMEM((WEIGHT_STAGES, d, ff_chunk), F32),
            pltpu.VMEM((WEIGHT_STAGES, d, ff_chunk), F32),
            pltpu.VMEM((WEIGHT_STAGES, ff_chunk, d), F32),
            pltpu.SemaphoreType.DMA,
            pltpu.SemaphoreType.DMA,
            pltpu.SemaphoreType.DMA((WEIGHT_STAGES,)),
        ],
    )
    return pl.pallas_call(
        _experts_kernel,
        grid_spec=grid_spec,
        out_shape=jax.ShapeDtypeStruct((n_tiles * tile_rows * slabs, LANES), F32),
        compiler_params=_params(("arbitrary",)),
        name="moe_experts",
    )(tile_expert, tile_valid, tile_first, tile_next, slot_tok, slot_tok, h_slab, w_gate_up, w_down, b_gate_up,
      b_down)


def _combine_kernel(dest_ref, destn_ref, gate_ref, ys_hbm, h_ref, g2_ref, b2_ref, os_ref, op_ref, idx_smem, ybuf,
                    fslab, fstd, sem_idx, sem_rows, *, n_sample_blocks, alpha):
    i = pl.program_id(0)
    n_blocks = pl.num_programs(0)
    tb, d = h_ref.shape
    slabs = d // LANES
    slot = lax.rem(i, 2)
    bufs = lambda s: [ybuf.at[s, k] for k in range(TOP_K)]

    @pl.when(i == 0)
    def _():
        _load_indices(dest_ref, idx_smem, sem_idx)
        _gather_start(ys_hbm, bufs(0), sem_rows.at[0], idx_smem, tb, slabs)

    @pl.when(i + 1 < n_blocks)
    def _():
        _load_indices(destn_ref, idx_smem, sem_idx)
        _gather_start(ys_hbm, bufs(1 - slot), sem_rows.at[1 - slot], idx_smem, tb, slabs)

    _gather_wait(ys_hbm, bufs(slot), sem_rows.at[slot], tb, slabs)

    gates = gate_ref[...]
    f = gates[:, 0:1] * ybuf[slot, 0]
    for k in range(1, TOP_K):
        f = f + gates[:, k:k + 1] * ybuf[slot, k]
    fslab[...] = f
    for kk in range(slabs):
        fstd[:, kk * LANES:(kk + 1) * LANES] = fslab[pl.ds(kk, tb, stride=slabs), :]
    y = _layer_norm(alpha * h_ref[...] + fstd[...], g2_ref[...], b2_ref[...])

    @pl.when(i < n_sample_blocks)
    def _():
        os_ref[...] = y

    @pl.when(i >= n_sample_blocks)
    def _():
        op_ref[...] = y


def _combine(dest, gates_slab, y_slab, h, g2, b2, *, n_sample, alpha, block):
    n, d = h.shape
    slabs = d // LANES
    nsb = n_sample // block
    assert block % ROW_DMA_UNROLL == 0
    kern = functools.partial(_combine_kernel, n_sample_blocks=nsb, alpha=alpha)
    const = lambda i: (0, 0)
    return pl.pallas_call(
        kern,
        grid=(n // block,),
        in_specs=[
            pl.BlockSpec((1, 1, block * TOP_K), lambda i: (i, 0, 0)),
            pl.BlockSpec((1, 1, block * TOP_K), lambda i: (jnp.minimum(i + 1, n // block - 1), 0, 0)),
            pl.BlockSpec((block * slabs, TOP_K), lambda i: (i, 0)),
            pl.BlockSpec(memory_space=pl.ANY),
            pl.BlockSpec((block, d), lambda i: (i, 0)),
            pl.BlockSpec((1, d), const),
            pl.BlockSpec((1, d), const),
        ],
        out_specs=[
            pl.BlockSpec((block, d), lambda i: (jnp.minimum(i, nsb - 1), 0)),
            pl.BlockSpec((block, d), lambda i: (jnp.maximum(i - nsb, 0), 0)),
        ],
        out_shape=[
            jax.ShapeDtypeStruct((n_sample, d), F32),
            jax.ShapeDtypeStruct((n - n_sample, d), F32),
        ],
        scratch_shapes=[
            pltpu.SMEM((block * TOP_K,), jnp.int32),
            pltpu.VMEM((2, TOP_K, block * slabs, LANES), F32),
            pltpu.VMEM((block * slabs, LANES), F32),
            pltpu.VMEM((block, d), F32),
            pltpu.SemaphoreType.DMA,
            pltpu.SemaphoreType.DMA((2,)),
        ],
        compiler_params=_params(("arbitrary",)),
        name="combine_norm",
    )(dest, dest, gates_slab, y_slab, h, g2, b2)


def _tiles(n_tokens, d_ff):
    return dict(
        token_block=min(256, n_tokens),
        route_block=min(512, n_tokens),
        expert_rows=min(512, n_tokens),
        ff_chunk=min(256, d_ff),
    )


def kernel(x_prompt, x_sample, ln_in_g, ln_in_b, w_in, rpb, w_pool, pool_scale, w_out, ln1_g, ln1_b, w_router,
           b_router, w_gate_up, b_gate_up, w_down, b_down, ln2_g, ln2_b):
    depth = w_in.shape[0]
    assert depth == 1, "single-layer trunk"
    d = x_prompt.shape[-1]
    prompt_len = x_prompt.shape[1]
    sample_len = x_sample.shape[1]
    xs = x_sample.reshape(-1, d)
    xp = x_prompt.reshape(-1, d)
    ns, np_ = xs.shape[0], xp.shape[0]
    n = ns + np_
    assert x_prompt.shape[0] == 1 and ns % prompt_len == 0
    n_heads = rpb.shape[1]
    attn_width = n_heads * HEAD_DIM
    n_experts = w_router.shape[-1]
    d_ff = w_down.shape[2]
    alpha = float((2 * depth) ** 0.25)
    t = _tiles(n, d_ff)
    tb = t["token_block"]
    slabs = d // LANES
    row = lambda v: v.reshape(1, -1).astype(F32)

    q, k, v, u = _in_proj(xs, xp, row(ln_in_g), row(ln_in_b), w_in[0].astype(BF16), attn_width=attn_width, block=tb)

    bias = _attn_bias_table(rpb[0])
    attn_s = _attention(q, k, v, bias, seq_len=sample_len, n_seq=ns // sample_len, first_block=0)
    attn_p = _attention(q, k, v, bias, seq_len=prompt_len, n_seq=1, first_block=ns // prompt_len)

    wr = jnp.pad(w_router[0].astype(F32), ((0, 0), (0, LANES - n_experts)))
    wr_hi = wr.astype(BF16)
    wr_lo = (wr - wr_hi.astype(F32)).astype(BF16)
    wr_cat = jnp.concatenate([wr_hi, wr_lo], axis=1)
    br = jnp.pad(b_router[0].astype(F32), (0, LANES - n_experts)).reshape(1, LANES)
    h, h_slab, logits = _mix(attn_s, attn_p, u, xs, xp, row(ln_in_g), row(ln_in_b), w_pool[0].astype(BF16),
                             row(pool_scale[0]), w_out[0].astype(BF16), row(ln1_g[0]), row(ln1_b[0]), wr_cat,
                             br, sample_len=sample_len, prompt_len=prompt_len, alpha=alpha, block=tb)

    meta, gates_pad, counts_pad = _route(logits, n_experts=n_experts, block=t["route_block"])
    top_idx = meta[:, :TOP_K]
    rank = meta[:, TOP_K:2 * TOP_K]
    gates = gates_pad[:, :TOP_K]
    counts = counts_pad[0, :n_experts]
    tm = t["expert_rows"]
    padded = (counts + tm - 1) // tm * tm
    pend = jnp.cumsum(padded)
    poff = pend - padded
    expert_ids = jnp.arange(n_experts, dtype=jnp.int32)
    base = jnp.sum(jnp.where(top_idx[:, :, None] == expert_ids, poff, 0), axis=-1)
    dest = (base + rank).astype(jnp.int32)
    n_tiles = (n * TOP_K) // tm + n_experts
    tok_of_pair = jnp.repeat(jnp.arange(n, dtype=jnp.int32), TOP_K)
    slot_tok = jnp.zeros((n_tiles * tm,), jnp.int32).at[dest.reshape(-1)].set(
        tok_of_pair, unique_indices=True, mode="promise_in_bounds")
    tile_start = jnp.arange(n_tiles, dtype=jnp.int32) * tm
    tile_expert = jnp.minimum(jnp.sum(tile_start[:, None] >= pend[None, :], axis=1), n_experts - 1).astype(jnp.int32)
    tile_valid = jnp.clip((poff + counts)[tile_expert] - tile_start, 0, tm).astype(jnp.int32)
    used = tile_valid > 0
    prev_expert = jnp.concatenate([jnp.full((1,), -1, jnp.int32), tile_expert[:-1]])
    tile_first = (used & (tile_expert != prev_expert)).astype(jnp.int32)
    next_first = jnp.concatenate([tile_first[1:], jnp.zeros((1,), jnp.int32)])
    next_expert = jnp.concatenate([tile_expert[1:], jnp.zeros((1,), jnp.int32)])
    tile_next = jnp.where(next_first > 0, next_expert, -1).astype(jnp.int32)

    y_slab = _experts(tile_expert, tile_valid, tile_first, tile_next, slot_tok.reshape(n_tiles, 1, tm), h_slab,
                      w_gate_up[0], b_gate_up[0].reshape(n_experts, -1, t["ff_chunk"]), w_down[0],
                      b_down[0].reshape(n_experts, 1, -1), tile_rows=tm, ff_chunk=t["ff_chunk"])

    gates_slab = jnp.repeat(gates, slabs, axis=0)
    out_s, out_p = _combine(dest.reshape(n // tb, 1, tb * TOP_K), gates_slab, y_slab, h,
                            row(ln2_g[0]), row(ln2_b[0]), n_sample=ns, alpha=alpha, block=tb)
    return out_p.reshape(x_prompt.shape), out_s.reshape(x_sample.shape)
```
